```python
import jax, jax.numpy as jnp
from jax import lax
import numpy as np

D_MODEL = 1024
BATCH = 16
SEQ = 4096
DEPTH = 1

FOX_HEAD_DIM = 64
FOX_WIDTH = D_MODEL // 2
FOX_HEADS = FOX_WIDTH // FOX_HEAD_DIM
Q_BLOCK = 128
MLSTM_HEADS = 4
MLSTM_WIDTH = D_MODEL // 2
MLSTM_HEAD_DIM = MLSTM_WIDTH // MLSTM_HEADS
CONV_WIDTH = 4
CHUNK = 128
D_FF = -(-8 * D_MODEL // (3 * 256)) * 256
EPS = 1e-6

IN_SPLIT_SIZES = (
    FOX_WIDTH, FOX_WIDTH, FOX_WIDTH, FOX_HEADS,
    MLSTM_WIDTH, MLSTM_WIDTH, MLSTM_WIDTH, MLSTM_HEADS, MLSTM_HEADS,
    MLSTM_WIDTH,
    D_MODEL, D_MODEL,
)
IN_WIDTH = sum(IN_SPLIT_SIZES)

kernel_name = 'fox_mlstm_gated_hybrid_block'


def rms_norm(x, g):
    xf = x.astype(jnp.float32)
    y = xf * lax.rsqrt(jnp.mean(xf * xf, axis=-1, keepdims=True) + EPS)
    return (y * g.astype(jnp.float32)).astype(x.dtype)


def head_layer_norm(x, g):
    B, S, H, d = x.shape
    xf = x.astype(jnp.float32)
    mu = jnp.mean(xf, axis=-1, keepdims=True)
    xc = xf - mu
    y = xc * lax.rsqrt(jnp.mean(xc * xc, axis=-1, keepdims=True) + EPS)
    return y.reshape(B, S, H * d) * g.astype(jnp.float32)


def causal_conv_silu(u, w, b):
    S = u.shape[1]
    up = jnp.pad(u, ((0, 0), (CONV_WIDTH - 1, 0), (0, 0)))
    y = b + sum(w[j] * up[:, j:j + S] for j in range(CONV_WIDTH))
    return jax.nn.silu(y)


def fox_attention(q, k, v, logf):
    B, S, H, d = q.shape
    nb = S // Q_BLOCK
    scale = d ** -0.5
    c = jnp.cumsum(logf, axis=1).transpose(0, 2, 1)
    q_blocks = q.reshape(B, nb, Q_BLOCK, H, d).transpose(1, 0, 2, 3, 4)
    c_blocks = c.reshape(B, H, nb, Q_BLOCK).transpose(2, 0, 1, 3)
    k_pos = jnp.arange(S)

    def block(args):
        qi, ci, bi = args
        s = jnp.einsum('bqhd,bkhd->bhqk', qi, k).astype(jnp.float32) * scale
        s = s + ci[..., :, None] - c[:, :, None, :]
        q_pos = bi * Q_BLOCK + jnp.arange(Q_BLOCK)
        s = jnp.where(k_pos[None, :] <= q_pos[:, None], s, -jnp.inf)
        p = jax.nn.softmax(s, axis=-1).astype(v.dtype)
        return jnp.einsum('bhqk,bkhd->bqhd', p, v)

    out = lax.map(block, (q_blocks, c_blocks, jnp.arange(nb)))
    return out.transpose(1, 0, 2, 3, 4).reshape(B, S, H, d)


def mlstm_chunkwise(q, k, v, logi, logf):
    B, S, H, d = q.shape
    nc = S // CHUNK
    f32 = jnp.float32

    def to_chunks(t):
        return t.astype(f32).reshape(B, nc, CHUNK, H, -1).transpose(1, 0, 3, 2, 4)

    qc = to_chunks(q)
    kc = to_chunks(k) * (d ** -0.5)
    vc = to_chunks(v)
    ic = to_chunks(logi[..., None])[..., 0]
    fc = to_chunks(logf[..., None])[..., 0]
    causal = jnp.tril(jnp.ones((CHUNK, CHUNK), dtype=bool))

    def step(carry, inp):
        C, n, m = carry
        qi, ki, vi, li, lf = inp
        b = jnp.cumsum(lf, axis=-1)
        a = b + m[..., None]
        D = jnp.where(causal, b[..., :, None] - b[..., None, :] + li[..., None, :], -jnp.inf)
        m_t = jnp.maximum(a, jnp.max(D, axis=-1))
        w_inter = jnp.exp(a - m_t)
        w_intra = jnp.exp(D - m_t[..., None])
        s = jnp.einsum('bhtd,bhsd->bhts', qi, ki) * w_intra
        num = w_inter[..., None] * jnp.einsum('bhvk,bhtk->bhtv', C, qi) + jnp.einsum('bhts,bhsv->bhtv', s, vi)
        den = w_inter * jnp.einsum('bhk,bhtk->bht', n, qi) + jnp.sum(s, axis=-1)
        h = num / jnp.maximum(jnp.abs(den), jnp.exp(-m_t))[..., None]
        m_new = m_t[..., -1]
        w_state = jnp.exp(b[..., -1:] - b + li - m_new[..., None])
        decay = jnp.exp(b[..., -1] + m - m_new)
        C_new = decay[..., None, None] * C + jnp.einsum('bhs,bhsv,bhsk->bhvk', w_state, vi, ki)
        n_new = decay[..., None] * n + jnp.einsum('bhs,bhsk->bhk', w_state, ki)
        return (C_new, n_new, m_new), h

    init = (jnp.zeros((B, H, d, d), f32), jnp.zeros((B, H, d), f32), jnp.zeros((B, H), f32))
    _, h = lax.scan(step, init, (qc, kc, vc, ic, fc))
    return h.transpose(1, 0, 3, 2, 4).reshape(B, S, H, d)


def hybrid_mixer(x, g_mix, w_in, b_fox_f, g_q_fox, g_k_fox, conv_w, conv_b,
                 b_mlstm_i, b_mlstm_f, g_mlstm_h, w_fox_out, w_mlstm_out, w_o):
    B, S, _ = x.shape
    h = rms_norm(x, g_mix)
    proj = h @ w_in
    split_points = [int(p) for p in np.cumsum(IN_SPLIT_SIZES)[:-1]]
    (qa, ka, va, fa, qb, kb, vb, ib, fb, ob, ga, gb) = jnp.split(proj, split_points, axis=-1)

    qa = rms_norm(qa.reshape(B, S, FOX_HEADS, FOX_HEAD_DIM), g_q_fox)
    ka = rms_norm(ka.reshape(B, S, FOX_HEADS, FOX_HEAD_DIM), g_k_fox)
    va = va.reshape(B, S, FOX_HEADS, FOX_HEAD_DIM)
    logf_a = jax.nn.log_sigmoid((fa + b_fox_f).astype(jnp.float32))
    ya = fox_attention(qa, ka, va, logf_a).reshape(B, S, FOX_WIDTH)

    qkb = causal_conv_silu(jnp.concatenate([qb, kb], axis=-1), conv_w, conv_b)
    qb, kb = jnp.split(qkb, 2, axis=-1)
    qb = qb.reshape(B, S, MLSTM_HEADS, MLSTM_HEAD_DIM)
    kb = kb.reshape(B, S, MLSTM_HEADS, MLSTM_HEAD_DIM)
    vb = vb.reshape(B, S, MLSTM_HEADS, MLSTM_HEAD_DIM)
    logi_b = (ib + b_mlstm_i).astype(jnp.float32)
    logf_b = jax.nn.log_sigmoid((fb + b_mlstm_f).astype(jnp.float32))
    hb = mlstm_chunkwise(qb, kb, vb, logi_b, logf_b)
    yb = (head_layer_norm(hb, g_mlstm_h) * jax.nn.sigmoid(ob.astype(jnp.float32))).astype(x.dtype)

    merged = jax.nn.sigmoid(ga) * (ya @ w_fox_out) + jax.nn.sigmoid(gb) * (yb @ w_mlstm_out)
    return merged @ w_o


def swiglu_ffn(x, g_ffn, w_gate, w_up, w_down):
    h = rms_norm(x, g_ffn)
    return (jax.nn.silu(h @ w_gate) * (h @ w_up)) @ w_down


def setup_inputs(seed: int = 0) -> dict:
    key = jax.random.key(seed)
    ks = jax.random.split(key, 20)

    def nrm(k, shape, scale):
        return jax.random.normal(k, shape, jnp.float32) * scale

    return {
        'x': nrm(ks[0], (BATCH, SEQ, D_MODEL), 1.0),
        'g_mix': 1.0 + nrm(ks[1], (DEPTH, D_MODEL), 0.02),
        'w_in': nrm(ks[2], (DEPTH, D_MODEL, IN_WIDTH), D_MODEL ** -0.5),
        'b_fox_f': jnp.linspace(1.0, 4.0, FOX_HEADS)[None, :] + nrm(ks[3], (DEPTH, FOX_HEADS), 0.01),
        'g_q_fox': 1.0 + nrm(ks[4], (DEPTH, FOX_HEAD_DIM), 0.02),
        'g_k_fox': 1.0 + nrm(ks[5], (DEPTH, FOX_HEAD_DIM), 0.02),
        'conv_w': nrm(ks[6], (DEPTH, CONV_WIDTH, 2 * MLSTM_WIDTH), CONV_WIDTH ** -0.5),
        'conv_b': nrm(ks[7], (DEPTH, 2 * MLSTM_WIDTH), 0.01),
        'b_mlstm_i': nrm(ks[8], (DEPTH, MLSTM_HEADS), 0.1),
        'b_mlstm_f': jnp.linspace(3.0, 6.0, MLSTM_HEADS)[None, :] + nrm(ks[9], (DEPTH, MLSTM_HEADS), 0.01),
        'g_mlstm_h': 1.0 + nrm(ks[10], (DEPTH, MLSTM_WIDTH), 0.02),
        'w_fox_out': nrm(ks[11], (DEPTH, FOX_WIDTH, D_MODEL), FOX_WIDTH ** -0.5),
        'w_mlstm_out': nrm(ks[12], (DEPTH, MLSTM_WIDTH, D_MODEL), MLSTM_WIDTH ** -0.5),
        'w_o': nrm(ks[13], (DEPTH, D_MODEL, D_MODEL), D_MODEL ** -0.5),
        'g_ffn': 1.0 + nrm(ks[14], (DEPTH, D_MODEL), 0.02),
        'w_gate': nrm(ks[15], (DEPTH, D_MODEL, D_FF), D_MODEL ** -0.5),
        'w_up': nrm(ks[16], (DEPTH, D_MODEL, D_FF), D_MODEL ** -0.5),
        'w_down': nrm(ks[17], (DEPTH, D_FF, D_MODEL), D_FF ** -0.5),
    }


def reference(x, g_mix, w_in, b_fox_f, g_q_fox, g_k_fox, conv_w, conv_b, b_mlstm_i, b_mlstm_f,
              g_mlstm_h, w_fox_out, w_mlstm_out, w_o, g_ffn, w_gate, w_up, w_down):
    for l in range(DEPTH):
        x = x + hybrid_mixer(x, g_mix[l], w_in[l], b_fox_f[l], g_q_fox[l], g_k_fox[l], conv_w[l], conv_b[l],
                             b_mlstm_i[l], b_mlstm_f[l], g_mlstm_h[l], w_fox_out[l], w_mlstm_out[l], w_o[l])
        x = x + swiglu_ffn(x, g_ffn[l], w_gate[l], w_up[l], w_down[l])
    return x
```

```python
import functools

import jax
import jax.numpy as jnp
from jax import lax
from jax.experimental import pallas as pl
from jax.experimental.pallas import tpu as pltpu

EPS = 1e-6
FOX_HEAD_DIM = 64
MLSTM_HEADS = 4
CONV_WIDTH = 4
LANES = 128
NEG_BIG = -1e30
VMEM_LIMIT_BYTES = 56 * 1024 * 1024

F32 = jnp.float32
BF16 = jnp.bfloat16


def _const_spec(shape):
    nd = len(shape)
    return pl.BlockSpec(shape, lambda *_: (0,) * nd, pipeline_mode=pl.Buffered(1))


def _log_sigmoid(z):
    return jnp.minimum(z, 0.0) - jnp.log(1.0 + jnp.exp(-jnp.abs(z)))


def _sigmoid(z):
    return 1.0 / (1.0 + jnp.exp(-z))


def _head_rms(x, g_row, lo_mask):
    outs = []
    for j in range(x.shape[1] // LANES):
        v = x[:, j * LANES:(j + 1) * LANES]
        v2 = v * v
        s_all = jnp.sum(v2, axis=-1, keepdims=True)
        s_lo = jnp.sum(jnp.where(lo_mask, v2, 0.0), axis=-1, keepdims=True)
        ms = jnp.where(lo_mask, s_lo, s_all - s_lo) * (1.0 / FOX_HEAD_DIM)
        outs.append(v * lax.rsqrt(ms + EPS))
    return jnp.concatenate(outs, axis=-1) * g_row


def _proj_kernel(x_ref, gmix_ref, w_ref, wg_ref, gb_ref, gq_ref, gk_ref, cw_ref, cb_ref,
                 qa_ref, ka_ref, va_ref, qb_ref, kbt_ref, vb_ref, og_ref, sga_ref, sgb_ref, gates_ref,
                 cbuf_ref, *, tm, fw, mw, d_model, k_scale):
    i = pl.program_id(1)
    x = x_ref[...]
    h = x * lax.rsqrt(jnp.mean(x * x, axis=-1, keepdims=True) + EPS) * gmix_ref[...]
    hb = h.astype(BF16)

    def proj(lo, width):
        return jnp.dot(hb, w_ref[:, lo:lo + width], preferred_element_type=F32)

    lane = lax.broadcasted_iota(jnp.int32, (1, LANES), 1)
    lo_mask = lane < FOX_HEAD_DIM

    off = 0
    qa_ref[...] = (_head_rms(proj(off, fw), gq_ref[...], lo_mask) * (FOX_HEAD_DIM ** -0.5)).astype(BF16)
    off += fw
    ka_ref[...] = _head_rms(proj(off, fw), gk_ref[...], lo_mask).astype(BF16)
    off += fw
    va_ref[...] = proj(off, fw).astype(BF16)
    off += fw

    @pl.when(i == 0)
    def _():
        cbuf_ref[0:8, :] = jnp.zeros((8, 2 * mw), F32)

    cbuf_ref[8:8 + tm, 0:mw] = proj(off, mw)
    off += mw
    cbuf_ref[8:8 + tm, mw:2 * mw] = proj(off, mw)
    off += mw
    for half in range(2):
        cols = slice(half * mw, (half + 1) * mw)
        y = cb_ref[:, cols] + cw_ref[CONV_WIDTH - 1:CONV_WIDTH, cols] * cbuf_ref[8:8 + tm, cols]
        for j in range(CONV_WIDTH - 1):
            r0 = 8 - (CONV_WIDTH - 1) + j
            y = y + cw_ref[j:j + 1, cols] * cbuf_ref[r0:r0 + tm, cols]
        y = y * _sigmoid(y)
        if half == 0:
            qb_ref[...] = y.astype(BF16)
        else:
            kbt_ref[...] = (y * k_scale).T.astype(BF16)
    cbuf_ref[0:8, :] = cbuf_ref[tm:tm + 8, :]

    vb_ref[...] = proj(off, mw).astype(BF16)
    off += mw
    og_ref[...] = _sigmoid(proj(off, mw)).astype(BF16)
    off += mw
    sga_ref[...] = _sigmoid(proj(off, d_model)).astype(BF16)
    off += d_model
    sgb_ref[...] = _sigmoid(proj(off, d_model)).astype(BF16)

    z = lax.dot_general(wg_ref[...], hb, (((1,), (1,)), ((), ())), preferred_element_type=F32) + gb_ref[...]
    row = lax.broadcasted_iota(jnp.int32, z.shape, 0)
    is_input_gate = (row >= 8) & (row < 12)
    gates_ref[...] = jnp.where(is_input_gate, z, _log_sigmoid(z))


def _proj_call(x, gmix, w_main, w_gt, gate_bias, gq, gk, conv_w, conv_b, *, tm):
    B, S, D = x.shape
    fw = gq.shape[1]
    mw = conv_w.shape[1] // 2
    n_main = w_main.shape[1]
    grid = (B, S // tm)
    tok = lambda width: pl.BlockSpec((None, tm, width), lambda b, i: (b, i, 0))
    out_shape = (
        jax.ShapeDtypeStruct((B, S, fw), BF16),
        jax.ShapeDtypeStruct((B, S, fw), BF16),
        jax.ShapeDtypeStruct((B, S, fw), BF16),
        jax.ShapeDtypeStruct((B, S, mw), BF16),
        jax.ShapeDtypeStruct((B, mw, S), BF16),
        jax.ShapeDtypeStruct((B, S, mw), BF16),
        jax.ShapeDtypeStruct((B, S, mw), BF16),
        jax.ShapeDtypeStruct((B, S, D), BF16),
        jax.ShapeDtypeStruct((B, S, D), BF16),
        jax.ShapeDtypeStruct((B, 16, S), F32),
    )
    out_specs = (
        tok(fw), tok(fw), tok(fw), tok(mw),
        pl.BlockSpec((None, mw, tm), lambda b, i: (b, 0, i)),
        tok(mw), tok(mw), tok(D), tok(D),
        pl.BlockSpec((None, 16, tm), lambda b, i: (b, 0, i)),
    )
    in_specs = [
        tok(D),
        _const_spec((1, D)),
        _const_spec((D, n_main)),
        _const_spec((16, D)),
        _const_spec((16, 1)),
        _const_spec((1, fw)),
        _const_spec((1, fw)),
        _const_spec((CONV_WIDTH, 2 * mw)),
        _const_spec((1, 2 * mw)),
    ]
    kern = functools.partial(_proj_kernel, tm=tm, fw=fw, mw=mw, d_model=D,
                             k_scale=(mw // MLSTM_HEADS) ** -0.5)
    return pl.pallas_call(
        kern, grid=grid, in_specs=in_specs, out_specs=out_specs, out_shape=out_shape,
        scratch_shapes=[pltpu.VMEM((tm + 8, 2 * mw), F32)],
        compiler_params=pltpu.CompilerParams(
            dimension_semantics=("arbitrary", "arbitrary"), vmem_limit_bytes=VMEM_LIMIT_BYTES),
        name="proj",
    )(x, gmix, w_main, w_gt, gate_bias, gq, gk, conv_w, conv_b)


def _scan_kernel(g_ref, o_ref, *, chunk):
    x = g_ref[...]
    S = x.shape[1]
    lane = lax.broadcasted_iota(jnp.int32, x.shape, 1)
    row = lax.broadcasted_iota(jnp.int32, x.shape, 0)
    glob = x
    sh = 1
    while sh < S:
        glob = glob + jnp.where(lane >= sh, pltpu.roll(glob, sh, axis=1), 0.0)
        sh *= 2
    seg = x
    lane_in = lane & (chunk - 1)
    sh = 1
    while sh < chunk:
        seg = seg + jnp.where(lane_in >= sh, pltpu.roll(seg, sh, axis=1), 0.0)
        sh *= 2
    o_ref[...] = jnp.where(row < 8, glob, jnp.where(row < 12, x, seg))


def _scan_call(gates, *, chunk):
    B, R, S = gates.shape
    spec = pl.BlockSpec((None, R, S), lambda b: (b, 0, 0))
    return pl.pallas_call(
        functools.partial(_scan_kernel, chunk=chunk),
        grid=(B,), in_specs=[spec], out_specs=spec,
        out_shape=jax.ShapeDtypeStruct(gates.shape, F32),
        compiler_params=pltpu.CompilerParams(dimension_semantics=("arbitrary",)),
        name="gate_scan",
    )(gates)


def _fox_kernel(q_ref, k_ref, v_ref, c_ref, o_ref, acc_ref, m_ref, l_ref, *, t):
    i = pl.program_id(2)
    lane = lax.broadcasted_iota(jnp.int32, (1, LANES), 1)
    lo_mask = lane < FOX_HEAD_DIM
    q2 = q_ref[...]
    zero = jnp.zeros_like(q2)
    qs = (jnp.where(lo_mask, q2, zero), jnp.where(lo_mask, zero, q2))
    c_tile = tuple(jnp.min(c_ref[hh, pl.ds(i, 1), :], axis=-1, keepdims=True) for hh in range(2))

    m_ref[...] = jnp.full(m_ref.shape, NEG_BIG, F32)
    l_ref[...] = jnp.zeros(l_ref.shape, F32)
    acc_ref[...] = jnp.zeros(acc_ref.shape, F32)

    reps = t // LANES

    def block(j, masked):
        start = pl.multiple_of(j * t, t)
        kb = k_ref[pl.ds(start, t), :]
        vb = v_ref[pl.ds(start, t), :]
        for hh in range(2):
            s = lax.dot_general(qs[hh], kb, (((1,), (1,)), ((), ())), preferred_element_type=F32)
            s = s - (c_ref[hh, pl.ds(j, 1), :] - c_tile[hh])
            if masked:
                r = lax.broadcasted_iota(jnp.int32, (t, t), 0)
                c = lax.broadcasted_iota(jnp.int32, (t, t), 1)
                s = jnp.where(c <= r, s, NEG_BIG)
            m_old = m_ref[hh]
            m_new = jnp.maximum(m_old, jnp.max(s, axis=1, keepdims=True))
            alpha = jnp.exp(m_old - m_new)
            p = jnp.exp(s - jnp.tile(m_new, (1, reps)))
            l_ref[hh] = alpha * l_ref[hh] + jnp.sum(p, axis=1, keepdims=True)
            acc_ref[hh] = alpha * acc_ref[hh] + jnp.dot(p.astype(BF16), vb, preferred_element_type=F32)
            m_ref[hh] = m_new

    def body(j, carry):
        block(j, False)
        return carry

    lax.fori_loop(0, i, body, 0)
    block(i, True)
    out = jnp.where(lo_mask, acc_ref[0] / l_ref[0], acc_ref[1] / l_ref[1])
    o_ref[...] = out.astype(o_ref.dtype)


def _fox_call(qa, ka, va, c5, *, t):
    B, S, W = qa.shape
    pairs = W // LANES
    grid = (B, pairs, S // t)
    qspec = pl.BlockSpec((None, t, LANES), lambda b, p, i: (b, i, p))
    kvspec = pl.BlockSpec((None, S, LANES), lambda b, p, i: (b, 0, p))
    cspec = pl.BlockSpec((None, None, 2, S // t, t), lambda b, p, i: (b, p, 0, 0, 0))
    return pl.pallas_call(
        functools.partial(_fox_kernel, t=t),
        grid=grid, in_specs=[qspec, kvspec, kvspec, cspec], out_specs=qspec,
        out_shape=jax.ShapeDtypeStruct((B, S, W), BF16),
        scratch_shapes=[pltpu.VMEM((2, t, LANES), F32), pltpu.VMEM((2, t, LANES), F32),
                        pltpu.VMEM((2, t, LANES), F32)],
        compiler_params=pltpu.CompilerParams(
            dimension_semantics=("arbitrary", "arbitrary", "arbitrary"), vmem_limit_bytes=VMEM_LIMIT_BYTES),
        name="fox_attn",
    )(qa, ka, va, c5)


def _mlstm_kernel(q_ref, kt_ref, v_ref, og_ref, b_ref, li_ref, gh_ref, o_ref, c_ref, m_ref, *, L, heads):
    ci = pl.program_id(1)
    d = LANES

    @pl.when(ci == 0)
    def _():
        c_ref[...] = jnp.zeros(c_ref.shape, F32)
        m_ref[...] = jnp.zeros(m_ref.shape, F32)

    r = lax.broadcasted_iota(jnp.int32, (L, L), 0)
    c = lax.broadcasted_iota(jnp.int32, (L, L), 1)
    causal = c <= r
    eye = c == r
    ones = jnp.ones((L, d), BF16)

    for hh in range(heads):
        cols = slice(hh * d, (hh + 1) * d)
        q = q_ref[:, cols]
        kt = kt_ref[cols, :]
        v_aug = jnp.concatenate([v_ref[:, cols], ones], axis=1)
        b_row = b_ref[hh, pl.ds(ci, 1), :]
        li_row = li_ref[hh, pl.ds(ci, 1), :]
        u_row = li_row - b_row
        b_col = jnp.sum(jnp.where(eye, b_row, 0.0), axis=1, keepdims=True)
        b_last = b_row[:, L - 1:L]
        m_prev = m_ref[hh]

        dmat = jnp.where(causal, b_col + u_row, NEG_BIG)
        m_d = jnp.max(dmat, axis=1, keepdims=True)
        s = jnp.dot(q, kt, preferred_element_type=F32) * jnp.exp(dmat - m_d)
        intra = jnp.dot(s.astype(BF16), v_aug, preferred_element_type=F32)

        c_state = c_ref[hh]
        inter = jnp.dot(q, c_state.astype(BF16), preferred_element_type=F32)

        a = b_col + m_prev
        m_t = jnp.maximum(a, m_d)
        w_inter = jnp.exp(a - m_t)
        w_intra = jnp.exp(m_d - m_t)
        num = w_inter * inter[:, :d] + w_intra * intra[:, :d]
        den = w_inter * inter[:, d:] + w_intra * intra[:, d:]
        hval = num / jnp.maximum(jnp.abs(den), jnp.exp(-m_t))

        u_max = jnp.max(u_row, axis=1, keepdims=True)
        w_row = jnp.exp(u_row - u_max)
        m_loc = b_last + u_max
        m_new = jnp.maximum(b_last + m_prev, m_loc)
        kw = (kt.astype(F32) * w_row).astype(BF16)
        upd = jnp.dot(kw, v_aug, preferred_element_type=F32)
        decay = jnp.exp(b_last + m_prev - m_new)
        gain = jnp.exp(m_loc - m_new)
        c_ref[hh] = jnp.tile(decay, (1, 2)) * c_state + jnp.tile(gain, (1, 2)) * upd
        m_ref[hh] = m_new

        mu = jnp.mean(hval, axis=-1, keepdims=True)
        xc = hval - mu
        yn = xc * lax.rsqrt(jnp.mean(xc * xc, axis=-1, keepdims=True) + EPS)
        o_ref[:, cols] = (yn * gh_ref[:, cols] * og_ref[:, cols].astype(F32)).astype(o_ref.dtype)


def _mlstm_call(qb, kbt, vb, og, bcs, li, gh, *, L):
    B, S, W = qb.shape
    heads = W // LANES
    grid = (B, S // L)
    tok = pl.BlockSpec((None, L, W), lambda b, ci: (b, ci, 0))
    ktspec = pl.BlockSpec((None, W, L), lambda b, ci: (b, 0, ci))
    gspec = pl.BlockSpec((None, heads, S // L, L), lambda b, ci: (b, 0, 0, 0))
    return pl.pallas_call(
        functools.partial(_mlstm_kernel, L=L, heads=heads),
        grid=grid, in_specs=[tok, ktspec, tok, tok, gspec, gspec, _const_spec((1, W))], out_specs=tok,
        out_shape=jax.ShapeDtypeStruct((B, S, W), BF16),
        scratch_shapes=[pltpu.VMEM((heads, LANES, 2 * LANES), F32), pltpu.VMEM((heads, 1, LANES), F32)],
        compiler_params=pltpu.CompilerParams(
            dimension_semantics=("arbitrary", "arbitrary"), vmem_limit_bytes=VMEM_LIMIT_BYTES),
        name="mlstm",
    )(qb, kbt, vb, og, bcs, li, gh)


def _post_kernel(x_ref, ya_ref, yb_ref, sga_ref, sgb_ref, wfo_ref, wmo_ref, wo_ref, gffn_ref,
                 wg_ref, wu_ref, wd_ref, o_ref):
    pa = jnp.dot(ya_ref[...], wfo_ref[...], preferred_element_type=F32)
    pb = jnp.dot(yb_ref[...], wmo_ref[...], preferred_element_type=F32)
    merged = sga_ref[...].astype(F32) * pa + sgb_ref[...].astype(F32) * pb
    x1 = x_ref[...] + jnp.dot(merged.astype(BF16), wo_ref[...], preferred_element_type=F32)
    h2 = (x1 * lax.rsqrt(jnp.mean(x1 * x1, axis=-1, keepdims=True) + EPS) * gffn_ref[...]).astype(BF16)
    gate = jnp.dot(h2, wg_ref[...], preferred_element_type=F32)
    up = jnp.dot(h2, wu_ref[...], preferred_element_type=F32)
    act = (gate * _sigmoid(gate) * up).astype(BF16)
    o_ref[...] = x1 + jnp.dot(act, wd_ref[...], preferred_element_type=F32)


def _post_call(x, ya, yb, sga, sgb, wfo, wmo, wo, gffn, wg, wu, wd, *, tm):
    B, S, D = x.shape
    tok = lambda width: pl.BlockSpec((None, tm, width), lambda b, i: (b, i, 0))
    in_specs = [tok(D), tok(ya.shape[2]), tok(yb.shape[2]), tok(D), tok(D),
                _const_spec(wfo.shape), _const_spec(wmo.shape), _const_spec(wo.shape), _const_spec(gffn.shape),
                _const_spec(wg.shape), _const_spec(wu.shape), _const_spec(wd.shape)]
    return pl.pallas_call(
        _post_kernel, grid=(B, S // tm), in_specs=in_specs, out_specs=tok(D),
        out_shape=jax.ShapeDtypeStruct((B, S, D), x.dtype),
        compiler_params=pltpu.CompilerParams(
            dimension_semantics=("arbitrary", "arbitrary"), vmem_limit_bytes=VMEM_LIMIT_BYTES),
        name="post",
    )(x, ya, yb, sga, sgb, wfo, wmo, wo, gffn, wg, wu, wd)


def _block(x, g_mix, w_in, b_fox_f, g_q_fox, g_k_fox, conv_w, conv_b, b_mlstm_i, b_mlstm_f,
           g_mlstm_h, w_fox_out, w_mlstm_out, w_o, g_ffn, w_gate, w_up, w_down):
    B, S, D = x.shape
    fw = w_fox_out.shape[0]
    mw = w_mlstm_out.shape[0]
    fh = fw // FOX_HEAD_DIM
    assert fh == 8 and mw // LANES == MLSTM_HEADS and fw % LANES == 0
    tm = min(512, S)
    t_attn = min(512, S)
    chunk = min(128, S)
    assert S % tm == 0 and S % t_attn == 0 and S % chunk == 0

    sizes = (fw, fw, fw, fh, mw, mw, mw, MLSTM_HEADS, MLSTM_HEADS, mw, D, D)
    offs = [0]
    for sz in sizes:
        offs.append(offs[-1] + sz)
    col = lambda k: w_in[:, offs[k]:offs[k + 1]]
    w_main = jnp.concatenate([col(0), col(1), col(2), col(4), col(5), col(6), col(9), col(10), col(11)],
                             axis=1).astype(BF16)
    w_gt = jnp.concatenate([col(3), col(7), col(8)], axis=1).T.astype(BF16)
    gate_bias = jnp.concatenate([b_fox_f, b_mlstm_i, b_mlstm_f]).astype(F32)[:, None]
    gq = jnp.tile(g_q_fox.astype(F32), fh)[None, :]
    gk = jnp.tile(g_k_fox.astype(F32), fh)[None, :]

    (qa, ka, va, qb, kbt, vb, og, sga, sgb, gates) = _proj_call(
        x, g_mix.astype(F32)[None, :], w_main, w_gt, gate_bias, gq, gk,
        conv_w.astype(F32), conv_b.astype(F32)[None, :], tm=tm)

    scans = _scan_call(gates, chunk=chunk)
    c5 = scans[:, :fh].reshape(B, fh // 2, 2, S // t_attn, t_attn)
    li = scans[:, fh:fh + MLSTM_HEADS].reshape(B, MLSTM_HEADS, S // chunk, chunk)
    bcs = scans[:, fh + MLSTM_HEADS:].reshape(B, MLSTM_HEADS, S // chunk, chunk)

    ya = _fox_call(qa, ka, va, c5, t=t_attn)
    yb = _mlstm_call(qb, kbt, vb, og, bcs, li, g_mlstm_h.astype(F32)[None, :], L=chunk)

    return _post_call(x, ya, yb, sga, sgb, w_fox_out.astype(BF16), w_mlstm_out.astype(BF16),
                      w_o.astype(BF16), g_ffn.astype(F32)[None, :], w_gate.astype(BF16),
                      w_up.astype(BF16), w_down.astype(BF16), tm=tm)


def kernel(x, g_mix, w_in, b_fox_f, g_q_fox, g_k_fox, conv_w, conv_b, b_mlstm_i, b_mlstm_f, g_mlstm_h,
           w_fox_out, w_mlstm_out, w_o, g_ffn, w_gate, w_up, w_down):
    for l in range(g_mix.shape[0]):
        x = _block(x, g_mix[l], w_in[l], b_fox_f[l], g_q_fox[l], g_k_fox[l], conv_w[l], conv_b[l],
                   b_mlstm_i[l], b_mlstm_f[l], g_mlstm_h[l], w_fox_out[l], w_mlstm_out[l], w_o[l],
                   g_ffn[l], w_gate[l], w_up[l], w_down[l])
    return x
```

```python
import functools

import jax
import jax.numpy as jnp
from jax import lax
from jax.experimental import pallas as pl
from jax.experimental.pallas import tpu as pltpu

EPS = 1e-6
FOX_HEAD_DIM = 64
MLSTM_HEADS = 4
CONV_WIDTH = 4
LANES = 128
NEG_BIG = -1e30
LOG2E = 1.4426950408889634
VMEM_LIMIT_BYTES = 56 * 1024 * 1024

F32 = jnp.float32
BF16 = jnp.bfloat16


def _const_spec(shape):
    nd = len(shape)
    return pl.BlockSpec(shape, lambda *_: (0,) * nd, pipeline_mode=pl.Buffered(1))


def _log_sigmoid(z):
    return jnp.minimum(z, 0.0) - jnp.log(1.0 + jnp.exp(-jnp.abs(z)))


def _sigmoid(z):
    return 1.0 / (1.0 + jnp.exp(-z))


def _head_rms(x, g_row, lo_mask):
    outs = []
    for j in range(x.shape[1] // LANES):
        v = x[:, j * LANES:(j + 1) * LANES]
        v2 = v * v
        s_all = jnp.sum(v2, axis=-1, keepdims=True)
        s_lo = jnp.sum(jnp.where(lo_mask, v2, 0.0), axis=-1, keepdims=True)
        ms = jnp.where(lo_mask, s_lo, s_all - s_lo) * (1.0 / FOX_HEAD_DIM)
        outs.append(v * lax.rsqrt(ms + EPS))
    return jnp.concatenate(outs, axis=-1) * g_row


def _proj_kernel(x_ref, gmix_ref, w_ref, wg_ref, gb_ref, gq_ref, gk_ref, cw_ref, cb_ref,
                 qa_ref, ka_ref, va_ref, qb_ref, kbt_ref, vb_ref, og_ref, sga_ref, sgb_ref, gates_ref,
                 cbuf_ref, *, tm, fw, mw, d_model, k_scale):
    i = pl.program_id(1)
    x = x_ref[...]
    h = x * lax.rsqrt(jnp.mean(x * x, axis=-1, keepdims=True) + EPS) * gmix_ref[...]
    hb = h.astype(BF16)

    def proj(lo, width):
        return jnp.dot(hb, w_ref[:, lo:lo + width], preferred_element_type=F32)

    lane = lax.broadcasted_iota(jnp.int32, (1, LANES), 1)
    lo_mask = lane < FOX_HEAD_DIM

    off = 0
    qa_ref[...] = (_head_rms(proj(off, fw), gq_ref[...], lo_mask) * (FOX_HEAD_DIM ** -0.5 * LOG2E)).astype(BF16)
    off += fw
    ka_ref[...] = _head_rms(proj(off, fw), gk_ref[...], lo_mask).astype(BF16)
    off += fw
    va_ref[...] = proj(off, fw).astype(BF16)
    off += fw

    @pl.when(i == 0)
    def _():
        cbuf_ref[0:8, :] = jnp.zeros((8, 2 * mw), F32)

    cbuf_ref[8:8 + tm, 0:mw] = proj(off, mw)
    off += mw
    cbuf_ref[8:8 + tm, mw:2 * mw] = proj(off, mw)
    off += mw
    for half in range(2):
        cols = slice(half * mw, (half + 1) * mw)
        y = cb_ref[:, cols] + cw_ref[CONV_WIDTH - 1:CONV_WIDTH, cols] * cbuf_ref[8:8 + tm, cols]
        for j in range(CONV_WIDTH - 1):
            r0 = 8 - (CONV_WIDTH - 1) + j
            y = y + cw_ref[j:j + 1, cols] * cbuf_ref[r0:r0 + tm, cols]
        y = y * _sigmoid(y)
        if half == 0:
            qb_ref[...] = y.astype(BF16)
        else:
            kbt_ref[...] = (y * k_scale).T.astype(BF16)
    cbuf_ref[0:8, :] = cbuf_ref[tm:tm + 8, :]

    vb_ref[...] = proj(off, mw).astype(BF16)
    off += mw
    og_ref[...] = _sigmoid(proj(off, mw)).astype(BF16)
    off += mw
    sga_ref[...] = _sigmoid(proj(off, d_model)).astype(BF16)
    off += d_model
    sgb_ref[...] = _sigmoid(proj(off, d_model)).astype(BF16)

    z = lax.dot_general(wg_ref[...], hb, (((1,), (1,)), ((), ())), preferred_element_type=F32) + gb_ref[...]
    row = lax.broadcasted_iota(jnp.int32, z.shape, 0)
    is_input_gate = (row >= 8) & (row < 12)
    gates_ref[...] = jnp.where(is_input_gate, z, _log_sigmoid(z))


def _proj_call(x, gmix, w_main, w_gt, gate_bias, gq, gk, conv_w, conv_b, *, tm):
    B, S, D = x.shape
    fw = gq.shape[1]
    mw = conv_w.shape[1] // 2
    n_main = w_main.shape[1]
    grid = (B, S // tm)
    tok = lambda width: pl.BlockSpec((None, tm, width), lambda b, i: (b, i, 0))
    out_shape = (
        jax.ShapeDtypeStruct((B, S, fw), BF16),
        jax.ShapeDtypeStruct((B, S, fw), BF16),
        jax.ShapeDtypeStruct((B, S, fw), BF16),
        jax.ShapeDtypeStruct((B, S, mw), BF16),
        jax.ShapeDtypeStruct((B, mw, S), BF16),
        jax.ShapeDtypeStruct((B, S, mw), BF16),
        jax.ShapeDtypeStruct((B, S, mw), BF16),
        jax.ShapeDtypeStruct((B, S, D), BF16),
        jax.ShapeDtypeStruct((B, S, D), BF16),
        jax.ShapeDtypeStruct((B, 16, S), F32),
    )
    out_specs = (
        tok(fw), tok(fw), tok(fw), tok(mw),
        pl.BlockSpec((None, mw, tm), lambda b, i: (b, 0, i)),
        tok(mw), tok(mw), tok(D), tok(D),
        pl.BlockSpec((None, 16, tm), lambda b, i: (b, 0, i)),
    )
    in_specs = [
        tok(D),
        _const_spec((1, D)),
        _const_spec((D, n_main)),
        _const_spec((16, D)),
        _const_spec((16, 1)),
        _const_spec((1, fw)),
        _const_spec((1, fw)),
        _const_spec((CONV_WIDTH, 2 * mw)),
        _const_spec((1, 2 * mw)),
    ]
    kern = functools.partial(_proj_kernel, tm=tm, fw=fw, mw=mw, d_model=D,
                             k_scale=(mw // MLSTM_HEADS) ** -0.5)
    return pl.pallas_call(
        kern, grid=grid, in_specs=in_specs, out_specs=out_specs, out_shape=out_shape,
        scratch_shapes=[pltpu.VMEM((tm + 8, 2 * mw), F32)],
        compiler_params=pltpu.CompilerParams(
            dimension_semantics=("arbitrary", "arbitrary"), vmem_limit_bytes=VMEM_LIMIT_BYTES),
        name="proj",
    )(x, gmix, w_main, w_gt, gate_bias, gq, gk, conv_w, conv_b)


def _scan_kernel(g_ref, o_ref, *, chunk):
    x = g_ref[...]
    S = x.shape[1]
    lane = lax.broadcasted_iota(jnp.int32, x.shape, 1)
    row = lax.broadcasted_iota(jnp.int32, x.shape, 0)
    glob = x
    sh = 1
    while sh < S:
        glob = glob + jnp.where(lane >= sh, pltpu.roll(glob, sh, axis=1), 0.0)
        sh *= 2
    seg = x
    lane_in = lane & (chunk - 1)
    sh = 1
    while sh < chunk:
        seg = seg + jnp.where(lane_in >= sh, pltpu.roll(seg, sh, axis=1), 0.0)
        sh *= 2
    o_ref[...] = jnp.where(row < 8, glob, jnp.where(row < 12, x, seg))


def _scan_call(gates, *, chunk):
    B, R, S = gates.shape
    spec = pl.BlockSpec((None, R, S), lambda b: (b, 0, 0))
    return pl.pallas_call(
        functools.partial(_scan_kernel, chunk=chunk),
        grid=(B,), in_specs=[spec], out_specs=spec,
        out_shape=jax.ShapeDtypeStruct(gates.shape, F32),
        compiler_params=pltpu.CompilerParams(dimension_semantics=("arbitrary",)),
        name="gate_scan",
    )(gates)


def _split3(x):
    hi = x.astype(BF16).astype(F32)
    r = x - hi
    mid = r.astype(BF16).astype(F32)
    lo = (r - mid).astype(BF16).astype(F32)
    return hi, mid, lo


def _fox_kernel(cs_ref, q_ref, k_ref, v_ref, c_ref, o_ref,
                kaug_ref, vtaug_ref, qt_ref, acc_ref, m_ref, s0_ref, s1_ref, *, t, nk):
    b = pl.program_id(0)
    p_idx = pl.program_id(1)
    i = pl.program_id(2)
    cs_base = (b * pl.num_programs(1) + p_idx) * 2 * nk
    half = FOX_HEAD_DIM

    row = lax.broadcasted_iota(jnp.int32, (LANES, t), 0)
    top = row < half
    lane = lax.broadcasted_iota(jnp.int32, (1, LANES), 1)
    lo_mask = lane < half

    q_t = q_ref[...].astype(F32).T
    ones_a = jnp.where(row < half + 3, 1.0, 0.0)
    ones_b = jnp.where(row < 3, 1.0, 0.0)
    qt_ref[0] = jnp.where(top, q_t, ones_a).astype(BF16)
    qt_ref[1] = jnp.where(top, ones_b, q_t).astype(BF16)

    m_ref[...] = jnp.full(m_ref.shape, NEG_BIG, F32)
    acc_ref[...] = jnp.zeros(acc_ref.shape, F32)

    def scores(j, s_ref):
        for hh in range(2):
            s_ref[hh] = jnp.dot(kaug_ref[hh, j], qt_ref[hh], preferred_element_type=F32)

    def consume(j, s_ref, masked):
        for hh in range(2):
            s = s_ref[hh]
            if masked:
                kr = lax.broadcasted_iota(jnp.int32, (t, t), 0)
                qc = lax.broadcasted_iota(jnp.int32, (t, t), 1)
                s = jnp.where(kr <= qc, s, NEG_BIG)
            d = (cs_ref[cs_base + hh * nk + j] - cs_ref[cs_base + hh * nk + i]) * LOG2E
            m_old = m_ref[hh]
            m_new = jnp.maximum(m_old, jnp.max(s, axis=0, keepdims=True) - d)
            alpha = jnp.exp2(m_old - m_new)
            p = jnp.exp2(s - (m_new + d))
            acc_ref[hh] = alpha * acc_ref[hh] + jnp.dot(vtaug_ref[hh, j], p.astype(BF16),
                                                       preferred_element_type=F32)
            m_ref[hh] = m_new

    row8 = lax.broadcasted_iota(jnp.int32, (8, t), 0)
    pieces = []
    for hh in range(2):
        brel = (cs_ref[cs_base + hh * nk + i] - c_ref[hh, pl.ds(i, 1), :]) * LOG2E
        hi, mid, lo = _split3(brel)
        pieces.append(jnp.where(row8 == 0, hi, jnp.where(row8 == 1, mid, jnp.where(row8 == 2, lo, 0.0))))
    pad = jnp.zeros((half - 8, t), F32)
    extra_t = jnp.concatenate([pieces[1], pad, pieces[0], pad], axis=0).T.astype(BF16)
    k2 = k_ref[...]
    kaug_ref[0, i] = jnp.where(lo_mask, k2, extra_t)
    kaug_ref[1, i] = jnp.where(lo_mask, extra_t, k2)
    v_t = v_ref[...].astype(F32).T.astype(BF16)
    one = jnp.ones_like(v_t)
    vtaug_ref[0, i] = jnp.where(top, v_t, one)
    vtaug_ref[1, i] = jnp.where(top, one, v_t)

    scores(0, s0_ref)

    def pair_body(n, carry):
        j = 2 * n
        scores(j + 1, s1_ref)
        consume(j, s0_ref, False)
        scores(j + 2, s0_ref)
        consume(j + 1, s1_ref, False)
        return carry

    lax.fori_loop(0, i // 2, pair_body, 0)

    @pl.when(i % 2 == 0)
    def _():
        consume(i, s0_ref, True)

    @pl.when(i % 2 == 1)
    def _():
        scores(i, s1_ref)
        consume(i - 1, s0_ref, False)
        consume(i, s1_ref, True)

    a0 = acc_ref[0]
    a1 = acc_ref[1]
    out_t = jnp.concatenate([a0[:half] / a0[half:], a1[half:] / a1[:half]], axis=0)
    o_ref[...] = out_t.T.astype(o_ref.dtype)


def _fox_call(qa, ka, va, c5, cstart, *, t):
    B, S, W = qa.shape
    pairs = W // LANES
    nk = S // t
    grid = (B, pairs, nk)
    tile = pl.BlockSpec((None, t, LANES), lambda b, p, i: (b, i, p))
    cspec = pl.BlockSpec((None, None, 2, nk, t), lambda b, p, i: (b, p, 0, 0, 0))
    return pl.pallas_call(
        functools.partial(_fox_kernel, t=t, nk=nk),
        grid=grid,
        in_specs=[pl.BlockSpec(memory_space=pltpu.SMEM), tile, tile, tile, cspec],
        out_specs=tile,
        out_shape=jax.ShapeDtypeStruct((B, S, W), BF16),
        scratch_shapes=[pltpu.VMEM((2, nk, t, LANES), BF16), pltpu.VMEM((2, nk, LANES, t), BF16),
                        pltpu.VMEM((2, LANES, t), BF16), pltpu.VMEM((2, LANES, t), F32),
                        pltpu.VMEM((2, 1, t), F32),
                        pltpu.VMEM((2, t, t), F32), pltpu.VMEM((2, t, t), F32)],
        compiler_params=pltpu.CompilerParams(
            dimension_semantics=("arbitrary", "arbitrary", "arbitrary"), vmem_limit_bytes=VMEM_LIMIT_BYTES),
        name="fox_attn",
    )(cstart, qa, ka, va, c5)


def _mlstm_kernel(q_ref, kt_ref, v_ref, og_ref, b_ref, li_ref, gh_ref, o_ref, c_ref, m_ref, *, L, heads):
    ci = pl.program_id(1)
    d = LANES

    @pl.when(ci == 0)
    def _():
        c_ref[...] = jnp.zeros(c_ref.shape, F32)
        m_ref[...] = jnp.zeros(m_ref.shape, F32)

    r = lax.broadcasted_iota(jnp.int32, (L, L), 0)
    c = lax.broadcasted_iota(jnp.int32, (L, L), 1)
    causal = c <= r
    eye = c == r
    ones = jnp.ones((L, d), BF16)

    for hh in range(heads):
        cols = slice(hh * d, (hh + 1) * d)
        q = q_ref[:, cols]
        kt = kt_ref[cols, :]
        v_aug = jnp.concatenate([v_ref[:, cols], ones], axis=1)
        b_row = b_ref[hh, pl.ds(ci, 1), :]
        li_row = li_ref[hh, pl.ds(ci, 1), :]
        u_row = li_row - b_row
        b_col = jnp.sum(jnp.where(eye, b_row, 0.0), axis=1, keepdims=True)
        b_last = b_row[:, L - 1:L]
        m_prev = m_ref[hh]

        dmat = jnp.where(causal, b_col + u_row, NEG_BIG)
        m_d = jnp.max(dmat, axis=1, keepdims=True)
        s = jnp.dot(q, kt, preferred_element_type=F32) * jnp.exp(dmat - m_d)
        intra = jnp.dot(s.astype(BF16), v_aug, preferred_element_type=F32)

        c_state = c_ref[hh]
        inter = jnp.dot(q, c_state.astype(BF16), preferred_element_type=F32)

        a = b_col + m_prev
        m_t = jnp.maximum(a, m_d)
        w_inter = jnp.exp(a - m_t)
        w_intra = jnp.exp(m_d - m_t)
        num = w_inter * inter[:, :d] + w_intra * intra[:, :d]
        den = w_inter * inter[:, d:] + w_intra * intra[:, d:]
        hval = num / jnp.maximum(jnp.abs(den), jnp.exp(-m_t))

        u_max = jnp.max(u_row, axis=1, keepdims=True)
        w_row = jnp.exp(u_row - u_max)
        m_loc = b_last + u_max
        m_new = jnp.maximum(b_last + m_prev, m_loc)
        kw = (kt.astype(F32) * w_row).astype(BF16)
        upd = jnp.dot(kw, v_aug, preferred_element_type=F32)
        decay = jnp.exp(b_last + m_prev - m_new)
        gain = jnp.exp(m_loc - m_new)
        c_ref[hh] = jnp.tile(decay, (1, 2)) * c_state + jnp.tile(gain, (1, 2)) * upd
        m_ref[hh] = m_new

        mu = jnp.mean(hval, axis=-1, keepdims=True)
        xc = hval - mu
        yn = xc * lax.rsqrt(jnp.mean(xc * xc, axis=-1, keepdims=True) + EPS)
        o_ref[:, cols] = (yn * gh_ref[:, cols] * og_ref[:, cols].astype(F32)).astype(o_ref.dtype)


def _mlstm_call(qb, kbt, vb, og, bcs, li, gh, *, L):
    B, S, W = qb.shape
    heads = W // LANES
    grid = (B, S // L)
    tok = pl.BlockSpec((None, L, W), lambda b, ci: (b, ci, 0))
    ktspec = pl.BlockSpec((None, W, L), lambda b, ci: (b, 0, ci))
    gspec = pl.BlockSpec((None, heads, S // L, L), lambda b, ci: (b, 0, 0, 0))
    return pl.pallas_call(
        functools.partial(_mlstm_kernel, L=L, heads=heads),
        grid=grid, in_specs=[tok, ktspec, tok, tok, gspec, gspec, _const_spec((1, W))], out_specs=tok,
        out_shape=jax.ShapeDtypeStruct((B, S, W), BF16),
        scratch_shapes=[pltpu.VMEM((heads, LANES, 2 * LANES), F32), pltpu.VMEM((heads, 1, LANES), F32)],
        compiler_params=pltpu.CompilerParams(
            dimension_semantics=("arbitrary", "arbitrary"), vmem_limit_bytes=VMEM_LIMIT_BYTES),
        name="mlstm",
    )(qb, kbt, vb, og, bcs, li, gh)


def _post_kernel(x_ref, ya_ref, yb_ref, sga_ref, sgb_ref, wfo_ref, wmo_ref, wo_ref, gffn_ref,
                 wg_ref, wu_ref, wd_ref, o_ref):
    pa = jnp.dot(ya_ref[...], wfo_ref[...], preferred_element_type=F32)
    pb = jnp.dot(yb_ref[...], wmo_ref[...], preferred_element_type=F32)
    merged = sga_ref[...].astype(F32) * pa + sgb_ref[...].astype(F32) * pb
    x1 = x_ref[...] + jnp.dot(merged.astype(BF16), wo_ref[...], preferred_element_type=F32)
    h2 = (x1 * lax.rsqrt(jnp.mean(x1 * x1, axis=-1, keepdims=True) + EPS) * gffn_ref[...]).astype(BF16)
    gate = jnp.dot(h2, wg_ref[...], preferred_element_type=F32)
    up = jnp.dot(h2, wu_ref[...], preferred_element_type=F32)
    act = (gate * _sigmoid(gate) * up).astype(BF16)
    o_ref[...] = x1 + jnp.dot(act, wd_ref[...], preferred_element_type=F32)


def _post_call(x, ya, yb, sga, sgb, wfo, wmo, wo, gffn, wg, wu, wd, *, tm):
    B, S, D = x.shape
    tok = lambda width: pl.BlockSpec((None, tm, width), lambda b, i: (b, i, 0))
    in_specs = [tok(D), tok(ya.shape[2]), tok(yb.shape[2]), tok(D), tok(D),
                _const_spec(wfo.shape), _const_spec(wmo.shape), _const_spec(wo.shape), _const_spec(gffn.shape),
                _const_spec(wg.shape), _const_spec(wu.shape), _const_spec(wd.shape)]
    return pl.pallas_call(
        _post_kernel, grid=(B, S // tm), in_specs=in_specs, out_specs=tok(D),
        out_shape=jax.ShapeDtypeStruct((B, S, D), x.dtype),
        compiler_params=pltpu.CompilerParams(
            dimension_semantics=("arbitrary", "arbitrary"), vmem_limit_bytes=VMEM_LIMIT_BYTES),
        name="post",
    )(x, ya, yb, sga, sgb, wfo, wmo, wo, gffn, wg, wu, wd)


def _block(x, g_mix, w_in, b_fox_f, g_q_fox, g_k_fox, conv_w, conv_b, b_mlstm_i, b_mlstm_f,
           g_mlstm_h, w_fox_out, w_mlstm_out, w_o, g_ffn, w_gate, w_up, w_down):
    B, S, D = x.shape
    fw = w_fox_out.shape[0]
    mw = w_mlstm_out.shape[0]
    fh = fw // FOX_HEAD_DIM
    assert fh == 8 and mw // LANES == MLSTM_HEADS and fw % LANES == 0
    tm = min(512, S)
    t_attn = min(512, S)
    chunk = min(128, S)
    assert S % tm == 0 and S % t_attn == 0 and S % chunk == 0

    sizes = (fw, fw, fw, fh, mw, mw, mw, MLSTM_HEADS, MLSTM_HEADS, mw, D, D)
    offs = [0]
    for sz in sizes:
        offs.append(offs[-1] + sz)
    col = lambda k: w_in[:, offs[k]:offs[k + 1]]
    w_main = jnp.concatenate([col(0), col(1), col(2), col(4), col(5), col(6), col(9), col(10), col(11)],
                             axis=1).astype(BF16)
    w_gt = jnp.concatenate([col(3), col(7), col(8)], axis=1).T.astype(BF16)
    gate_bias = jnp.concatenate([b_fox_f, b_mlstm_i, b_mlstm_f]).astype(F32)[:, None]
    gq = jnp.tile(g_q_fox.astype(F32), fh)[None, :]
    gk = jnp.tile(g_k_fox.astype(F32), fh)[None, :]

    (qa, ka, va, qb, kbt, vb, og, sga, sgb, gates) = _proj_call(
        x, g_mix.astype(F32)[None, :], w_main, w_gt, gate_bias, gq, gk,
        conv_w.astype(F32), conv_b.astype(F32)[None, :], tm=tm)

    scans = _scan_call(gates, chunk=chunk)
    c5 = scans[:, :fh].reshape(B, fh // 2, 2, S // t_attn, t_attn)
    li = scans[:, fh:fh + MLSTM_HEADS].reshape(B, MLSTM_HEADS, S // chunk, chunk)
    bcs = scans[:, fh + MLSTM_HEADS:].reshape(B, MLSTM_HEADS, S // chunk, chunk)

    cstart = c5[..., 0].reshape(-1)
    ya = _fox_call(qa, ka, va, c5, cstart, t=t_attn)
    yb = _mlstm_call(qb, kbt, vb, og, bcs, li, g_mlstm_h.astype(F32)[None, :], L=chunk)

    return _post_call(x, ya, yb, sga, sgb, w_fox_out.astype(BF16), w_mlstm_out.astype(BF16),
                      w_o.astype(BF16), g_ffn.astype(F32)[None, :], w_gate.astype(BF16),
                      w_up.astype(BF16), w_down.astype(BF16), tm=tm)


def kernel(x, g_mix, w_in, b_fox_f, g_q_fox, g_k_fox, conv_w, conv_b, b_mlstm_i, b_mlstm_f, g_mlstm_h,
           w_fox_out, w_mlstm_out, w_o, g_ffn, w_gate, w_up, w_down):
    for l in range(g_mix.shape[0]):
        x = _block(x, g_mix[l], w_in[l], b_fox_f[l], g_q_fox[l], g_k_fox[l], conv_w[l], conv_b[l],
                   b_mlstm_i[l], b_mlstm_f[l], g_mlstm_h[l], w_fox_out[l], w_mlstm_out[l], w_o[l],
                   g_ffn[l], w_gate[l], w_up[l], w_down[l])
    return x
```

```python
import functools

import jax
import jax.numpy as jnp
from jax import lax
from jax.experimental import pallas as pl
from jax.experimental.pallas import tpu as pltpu

EPS = 1e-6
FOX_HEAD_DIM = 64
MLSTM_HEADS = 4
CONV_WIDTH = 4
LANES = 128
NEG_BIG = -1e30
LOG2E = 1.4426950408889634
SAFE_EXP2_RANGE = 100.0
VMEM_LIMIT_BYTES = 56 * 1024 * 1024

F32 = jnp.float32
BF16 = jnp.bfloat16


def _const_spec(shape):
    nd = len(shape)
    return pl.BlockSpec(shape, lambda *_: (0,) * nd, pipeline_mode=pl.Buffered(1))


def _log_sigmoid(z):
    return jnp.minimum(z, 0.0) - jnp.log(1.0 + jnp.exp(-jnp.abs(z)))


def _sigmoid(z):
    return 1.0 / (1.0 + jnp.exp(-z))


def _head_rms(x, g_row, lo_mask):
    outs = []
    for j in range(x.shape[1] // LANES):
        v = x[:, j * LANES:(j + 1) * LANES]
        v2 = v * v
        s_lo = jnp.sum(jnp.where(lo_mask, v2, 0.0), axis=-1, keepdims=True)
        s_hi = jnp.sum(jnp.where(lo_mask, 0.0, v2), axis=-1, keepdims=True)
        ms = jnp.where(lo_mask, s_lo, s_hi) * (1.0 / FOX_HEAD_DIM)
        outs.append(v * lax.rsqrt(ms + EPS))
    return jnp.concatenate(outs, axis=-1) * g_row


def _proj_kernel(x_ref, gmix_ref, w_ref, wg_ref, gb_ref, gq_ref, gk_ref, cw_ref, cb_ref,
                 qa_ref, ka_ref, va_ref, qb_ref, kbt_ref, vb_ref, og_ref, sga_ref, sgb_ref, gates_ref,
                 cbuf_ref, *, tm, fw, mw, d_model, k_scale):
    i = pl.program_id(1)
    x = x_ref[...]
    h = x * lax.rsqrt(jnp.mean(x * x, axis=-1, keepdims=True) + EPS) * gmix_ref[...]
    hb = h.astype(BF16)

    def proj(lo, width):
        return jnp.dot(hb, w_ref[:, lo:lo + width], preferred_element_type=F32)

    lane = lax.broadcasted_iota(jnp.int32, (1, LANES), 1)
    lo_mask = lane < FOX_HEAD_DIM

    off = 0
    qa_ref[...] = (_head_rms(proj(off, fw), gq_ref[...], lo_mask) * (FOX_HEAD_DIM ** -0.5 * LOG2E)).astype(BF16)
    off += fw
    ka_ref[...] = _head_rms(proj(off, fw), gk_ref[...], lo_mask).astype(BF16)
    off += fw
    va_ref[...] = proj(off, fw).astype(BF16)
    off += fw

    @pl.when(i == 0)
    def _():
        cbuf_ref[0:8, :] = jnp.zeros((8, 2 * mw), F32)

    cbuf_ref[8:8 + tm, 0:mw] = proj(off, mw)
    off += mw
    cbuf_ref[8:8 + tm, mw:2 * mw] = proj(off, mw)
    off += mw

    vb_ref[...] = proj(off, mw).astype(BF16)
    off += mw
    og_ref[...] = _sigmoid(proj(off, mw)).astype(BF16)
    off += mw
    sga_ref[...] = _sigmoid(proj(off, d_model)).astype(BF16)
    off += d_model
    sgb_ref[...] = _sigmoid(proj(off, d_model)).astype(BF16)

    z = lax.dot_general(wg_ref[...], hb, (((1,), (1,)), ((), ())), preferred_element_type=F32) + gb_ref[...]
    row = lax.broadcasted_iota(jnp.int32, z.shape, 0)
    is_input_gate = (row >= 8) & (row < 12)
    gates_ref[...] = jnp.where(is_input_gate, z, _log_sigmoid(z))

    for half in range(2):
        cols = slice(half * mw, (half + 1) * mw)
        y = cb_ref[:, cols] + cw_ref[CONV_WIDTH - 1:CONV_WIDTH, cols] * cbuf_ref[8:8 + tm, cols]
        for j in range(CONV_WIDTH - 1):
            r0 = 8 - (CONV_WIDTH - 1) + j
            y = y + cw_ref[j:j + 1, cols] * cbuf_ref[r0:r0 + tm, cols]
        y = y * _sigmoid(y)
        if half == 0:
            qb_ref[...] = y.astype(BF16)
        else:
            kbt_ref[...] = (y * k_scale).T.astype(BF16)
    cbuf_ref[0:8, :] = cbuf_ref[tm:tm + 8, :]


def _proj_call(x, gmix, w_main, w_gt, gate_bias, gq, gk, conv_w, conv_b, *, tm):
    B, S, D = x.shape
    fw = gq.shape[1]
    mw = conv_w.shape[1] // 2
    n_main = w_main.shape[1]
    grid = (B, S // tm)
    tok = lambda width: pl.BlockSpec((None, tm, width), lambda b, i: (b, i, 0))
    out_shape = (
        jax.ShapeDtypeStruct((B, S, fw), BF16),
        jax.ShapeDtypeStruct((B, S, fw), BF16),
        jax.ShapeDtypeStruct((B, S, fw), BF16),
        jax.ShapeDtypeStruct((B, S, mw), BF16),
        jax.ShapeDtypeStruct((B, mw, S), BF16),
        jax.ShapeDtypeStruct((B, S, mw), BF16),
        jax.ShapeDtypeStruct((B, S, mw), BF16),
        jax.ShapeDtypeStruct((B, S, D), BF16),
        jax.ShapeDtypeStruct((B, S, D), BF16),
        jax.ShapeDtypeStruct((B, 16, S), F32),
    )
    out_specs = (
        tok(fw), tok(fw), tok(fw), tok(mw),
        pl.BlockSpec((None, mw, tm), lambda b, i: (b, 0, i)),
        tok(mw), tok(mw), tok(D), tok(D),
        pl.BlockSpec((None, 16, tm), lambda b, i: (b, 0, i)),
    )
    in_specs = [
        tok(D),
        _const_spec((1, D)),
        _const_spec((D, n_main)),
        _const_spec((16, D)),
        _const_spec((16, 1)),
        _const_spec((1, fw)),
        _const_spec((1, fw)),
        _const_spec((CONV_WIDTH, 2 * mw)),
        _const_spec((1, 2 * mw)),
    ]
    kern = functools.partial(_proj_kernel, tm=tm, fw=fw, mw=mw, d_model=D,
                             k_scale=(mw // MLSTM_HEADS) ** -0.5)
    return pl.pallas_call(
        kern, grid=grid, in_specs=in_specs, out_specs=out_specs, out_shape=out_shape,
        scratch_shapes=[pltpu.VMEM((tm + 8, 2 * mw), F32)],
        compiler_params=pltpu.CompilerParams(
            dimension_semantics=("arbitrary", "arbitrary"), vmem_limit_bytes=VMEM_LIMIT_BYTES),
        name="proj",
    )(x, gmix, w_main, w_gt, gate_bias, gq, gk, conv_w, conv_b)


def _scan_kernel(g_ref, o_ref, *, chunk):
    x = g_ref[...]
    S = x.shape[1]
    lane = lax.broadcasted_iota(jnp.int32, x.shape, 1)
    row = lax.broadcasted_iota(jnp.int32, x.shape, 0)
    glob = x
    sh = 1
    while sh < S:
        glob = glob + jnp.where(lane >= sh, pltpu.roll(glob, sh, axis=1), 0.0)
        sh *= 2
    seg = x
    lane_in = lane & (chunk - 1)
    sh = 1
    while sh < chunk:
        seg = seg + jnp.where(lane_in >= sh, pltpu.roll(seg, sh, axis=1), 0.0)
        sh *= 2
    o_ref[...] = jnp.where(row < 8, glob, jnp.where(row < 12, x, seg))


def _scan_call(gates, *, chunk):
    B, R, S = gates.shape
    spec = pl.BlockSpec((None, R, S), lambda b: (b, 0, 0))
    return pl.pallas_call(
        functools.partial(_scan_kernel, chunk=chunk),
        grid=(B,), in_specs=[spec], out_specs=spec,
        out_shape=jax.ShapeDtypeStruct(gates.shape, F32),
        compiler_params=pltpu.CompilerParams(dimension_semantics=("arbitrary",)),
        name="gate_scan",
    )(gates)


def _split3(x):
    hi = x.astype(BF16).astype(F32)
    r = x - hi
    mid = r.astype(BF16).astype(F32)
    lo = (r - mid).astype(BF16).astype(F32)
    return hi, mid, lo


def _fox_kernel(cs_ref, q_ref, k_ref, v_ref, c_ref, o_ref,
                kaug_ref, vtaug_ref, qt_ref, acc_ref, m_ref, s0_ref, s1_ref, p0_ref, p1_ref, *, t, nk):
    b = pl.program_id(0)
    p_idx = pl.program_id(1)
    i = pl.program_id(2)
    cs_base = (b * pl.num_programs(1) + p_idx) * 2 * nk
    half = FOX_HEAD_DIM

    row = lax.broadcasted_iota(jnp.int32, (LANES, t), 0)
    top = row < half
    lane = lax.broadcasted_iota(jnp.int32, (1, LANES), 1)
    lo_mask = lane < half

    q_t = q_ref[...].astype(F32).T
    ones_a = jnp.where(row < half + 3, 1.0, 0.0)
    ones_b = jnp.where(row < 3, 1.0, 0.0)
    qt_ref[0] = jnp.where(top, q_t, ones_a).astype(BF16)
    qt_ref[1] = jnp.where(top, ones_b, q_t).astype(BF16)

    acc_ref[...] = jnp.zeros(acc_ref.shape, F32)

    def causal_mask(s):
        kr = lax.broadcasted_iota(jnp.int32, (t, t), 0)
        qc = lax.broadcasted_iota(jnp.int32, (t, t), 1)
        return jnp.where(kr <= qc, s, NEG_BIG)

    def pexp(j, p_ref, masked):
        for hh in range(2):
            s = jnp.dot(kaug_ref[hh, j], qt_ref[hh], preferred_element_type=F32)
            if masked:
                s = causal_mask(s)
            shift = (cs_ref[cs_base + hh * nk + j] - c_q[hh]) * LOG2E + bound
            p_ref[hh] = jnp.exp2(s - shift).astype(BF16)

    def pv(j, p_ref):
        for hh in range(2):
            acc_ref[hh] += jnp.dot(vtaug_ref[hh, j], p_ref[hh], preferred_element_type=F32)

    def scores(j, s_ref):
        for hh in range(2):
            s_ref[hh] = jnp.dot(kaug_ref[hh, j], qt_ref[hh], preferred_element_type=F32)

    def consume(j, s_ref, masked):
        for hh in range(2):
            s = s_ref[hh]
            if masked:
                s = causal_mask(s)
            d = (cs_ref[cs_base + hh * nk + j] - cs_ref[cs_base + hh * nk + i]) * LOG2E
            m_old = m_ref[hh]
            m_new = jnp.maximum(m_old, jnp.max(s, axis=0, keepdims=True) - d)
            alpha = jnp.exp2(m_old - m_new)
            p = jnp.exp2(s - (m_new + d))
            acc_ref[hh] = alpha * acc_ref[hh] + jnp.dot(vtaug_ref[hh, j], p.astype(BF16),
                                                       preferred_element_type=F32)
            m_ref[hh] = m_new

    row8 = lax.broadcasted_iota(jnp.int32, (8, t), 0)
    pieces = []
    for hh in range(2):
        brel = (cs_ref[cs_base + hh * nk + i] - c_ref[hh, pl.ds(i, 1), :]) * LOG2E
        hi, mid, lo = _split3(brel)
        pieces.append(jnp.where(row8 == 0, hi, jnp.where(row8 == 1, mid, jnp.where(row8 == 2, lo, 0.0))))
    pad = jnp.zeros((half - 8, t), F32)
    extra_t = jnp.concatenate([pieces[1], pad, pieces[0], pad], axis=0).T.astype(BF16)
    k2 = k_ref[...]
    kaug_ref[0, i] = jnp.where(lo_mask, k2, extra_t)
    kaug_ref[1, i] = jnp.where(lo_mask, extra_t, k2)
    v_t = v_ref[...].astype(F32).T.astype(BF16)
    one = jnp.ones_like(v_t)
    vtaug_ref[0, i] = jnp.where(top, v_t, one)
    vtaug_ref[1, i] = jnp.where(top, one, v_t)

    bound = cs_ref[pl.num_programs(0) * pl.num_programs(1) * 2 * nk]
    c_q = [c_ref[hh, pl.ds(i, 1), :] for hh in range(2)]
    safe = 2.0 * bound < SAFE_EXP2_RANGE

    @pl.when(safe)
    def _():
        pexp(i, p0_ref, True)

        def pair_body(n, carry):
            j = 2 * n
            pexp(j, p1_ref, False)
            pv(jnp.where(n == 0, i, j - 1), p0_ref)
            pexp(j + 1, p0_ref, False)
            pv(j, p1_ref)
            return carry

        lax.fori_loop(0, i // 2, pair_body, 0)

        @pl.when(i % 2 == 0)
        def _():
            pv(jnp.where(i == 0, i, i - 1), p0_ref)

        @pl.when(i % 2 == 1)
        def _():
            pexp(i - 1, p1_ref, False)
            pv(jnp.where(i == 1, i, i - 2), p0_ref)
            pv(i - 1, p1_ref)

    @pl.when(jnp.logical_not(safe))
    def _():
        m_ref[...] = jnp.full(m_ref.shape, NEG_BIG, F32)
        scores(0, s0_ref)

        def pair_body(n, carry):
            j = 2 * n
            scores(j + 1, s1_ref)
            consume(j, s0_ref, False)
            scores(j + 2, s0_ref)
            consume(j + 1, s1_ref, False)
            return carry

        lax.fori_loop(0, i // 2, pair_body, 0)

        @pl.when(i % 2 == 0)
        def _():
            consume(i, s0_ref, True)

        @pl.when(i % 2 == 1)
        def _():
            scores(i, s1_ref)
            consume(i - 1, s0_ref, False)
            consume(i, s1_ref, True)

    a0 = acc_ref[0]
    a1 = acc_ref[1]
    out_t = jnp.concatenate([a0[:half] / a0[half:], a1[half:] / a1[:half]], axis=0)
    o_ref[...] = out_t.T.astype(o_ref.dtype)


def _fox_call(qa, ka, va, c5, cstart, *, t):
    B, S, W = qa.shape
    pairs = W // LANES
    nk = S // t
    grid = (B, pairs, nk)
    tile = pl.BlockSpec((None, t, LANES), lambda b, p, i: (b, i, p))
    cspec = pl.BlockSpec((None, None, 2, nk, t), lambda b, p, i: (b, p, 0, 0, 0))
    return pl.pallas_call(
        functools.partial(_fox_kernel, t=t, nk=nk),
        grid=grid,
        in_specs=[pl.BlockSpec(memory_space=pltpu.SMEM), tile, tile, tile, cspec],
        out_specs=tile,
        out_shape=jax.ShapeDtypeStruct((B, S, W), BF16),
        scratch_shapes=[pltpu.VMEM((2, nk, t, LANES), BF16), pltpu.VMEM((2, nk, LANES, t), BF16),
                        pltpu.VMEM((2, LANES, t), BF16), pltpu.VMEM((2, LANES, t), F32),
                        pltpu.VMEM((2, 1, t), F32),
                        pltpu.VMEM((2, t, t), F32), pltpu.VMEM((2, t, t), F32),
                        pltpu.VMEM((2, t, t), BF16), pltpu.VMEM((2, t, t), BF16)],
        compiler_params=pltpu.CompilerParams(
            dimension_semantics=("arbitrary", "arbitrary", "arbitrary"), vmem_limit_bytes=VMEM_LIMIT_BYTES),
        name="fox_attn",
    )(cstart, qa, ka, va, c5)


def _mlstm_kernel(q_ref, kt_ref, v_ref, og_ref, b_ref, li_ref, gh_ref, o_ref, c_ref, m_ref, *, L, heads, nchunk):
    step = pl.program_id(1)
    d = LANES

    @pl.when(step == 0)
    def _():
        c_ref[...] = jnp.zeros(c_ref.shape, F32)
        m_ref[...] = jnp.zeros(m_ref.shape, F32)

    r = lax.broadcasted_iota(jnp.int32, (L, L), 0)
    c = lax.broadcasted_iota(jnp.int32, (L, L), 1)
    causal = c <= r
    eye = c == r
    ones = jnp.ones((L, d), BF16)
    hs = range(heads)
    col = lambda hh: slice(hh * d, (hh + 1) * d)

    c_state = [c_ref[hh] for hh in hs]
    m_prev = [m_ref[hh] for hh in hs]

    for cc in range(nchunk):
        rows = slice(cc * L, (cc + 1) * L)
        ci = step * nchunk + cc
        q = [q_ref[rows, col(hh)] for hh in hs]
        kt = [kt_ref[col(hh), rows] for hh in hs]
        v_aug = [jnp.concatenate([v_ref[rows, col(hh)], ones], axis=1) for hh in hs]

        s_raw = [jnp.dot(q[hh], kt[hh], preferred_element_type=F32) for hh in hs]
        inter = [jnp.dot(q[hh], c_state[hh].astype(BF16), preferred_element_type=F32) for hh in hs]

        b_row = [b_ref[hh, pl.ds(ci, 1), :] for hh in hs]
        u_row = [li_ref[hh, pl.ds(ci, 1), :] - b_row[hh] for hh in hs]
        b_last = [b_row[hh][:, L - 1:L] for hh in hs]
        u_max = [jnp.max(u_row[hh], axis=1, keepdims=True) for hh in hs]
        m_loc = [b_last[hh] + u_max[hh] for hh in hs]
        kw = [(kt[hh].astype(F32) * jnp.exp(u_row[hh] - u_max[hh])).astype(BF16) for hh in hs]
        upd = [jnp.dot(kw[hh], v_aug[hh], preferred_element_type=F32) for hh in hs]

        b_col = [jnp.sum(jnp.where(eye, b_row[hh], 0.0), axis=1, keepdims=True) for hh in hs]
        dmat = [jnp.where(causal, b_col[hh] + u_row[hh], NEG_BIG) for hh in hs]
        m_d = [jnp.max(dmat[hh], axis=1, keepdims=True) for hh in hs]
        s = [(s_raw[hh] * jnp.exp(dmat[hh] - m_d[hh])).astype(BF16) for hh in hs]
        intra = [jnp.dot(s[hh], v_aug[hh], preferred_element_type=F32) for hh in hs]

        for hh in hs:
            a = b_col[hh] + m_prev[hh]
            m_t = jnp.maximum(a, m_d[hh])
            w_inter = jnp.exp(a - m_t)
            w_intra = jnp.exp(m_d[hh] - m_t)
            num = w_inter * inter[hh][:, :d] + w_intra * intra[hh][:, :d]
            den = w_inter * inter[hh][:, d:] + w_intra * intra[hh][:, d:]
            hval = num / jnp.maximum(jnp.abs(den), jnp.exp(-m_t))

            m_new = jnp.maximum(b_last[hh] + m_prev[hh], m_loc[hh])
            decay = jnp.exp(b_last[hh] + m_prev[hh] - m_new)
            gain = jnp.exp(m_loc[hh] - m_new)
            c_state[hh] = jnp.tile(decay, (1, 2)) * c_state[hh] + jnp.tile(gain, (1, 2)) * upd[hh]
            m_prev[hh] = m_new

            mu = jnp.mean(hval, axis=-1, keepdims=True)
            xc = hval - mu
            yn = xc * lax.rsqrt(jnp.mean(xc * xc, axis=-1, keepdims=True) + EPS)
            o_ref[rows, col(hh)] = (yn * gh_ref[:, col(hh)]
                                    * og_ref[rows, col(hh)].astype(F32)).astype(o_ref.dtype)

    for hh in hs:
        c_ref[hh] = c_state[hh]
        m_ref[hh] = m_prev[hh]


def _mlstm_call(qb, kbt, vb, og, bcs, li, gh, *, L, nchunk):
    B, S, W = qb.shape
    heads = W // LANES
    T = L * nchunk
    grid = (B, S // T)
    tok = pl.BlockSpec((None, T, W), lambda b, ci: (b, ci, 0))
    ktspec = pl.BlockSpec((None, W, T), lambda b, ci: (b, 0, ci))
    gspec = pl.BlockSpec((None, heads, S // L, L), lambda b, ci: (b, 0, 0, 0))
    return pl.pallas_call(
        functools.partial(_mlstm_kernel, L=L, heads=heads, nchunk=nchunk),
        grid=grid, in_specs=[tok, ktspec, tok, tok, gspec, gspec, _const_spec((1, W))], out_specs=tok,
        out_shape=jax.ShapeDtypeStruct((B, S, W), BF16),
        scratch_shapes=[pltpu.VMEM((heads, LANES, 2 * LANES), F32), pltpu.VMEM((heads, 1, LANES), F32)],
        compiler_params=pltpu.CompilerParams(
            dimension_semantics=("arbitrary", "arbitrary"), vmem_limit_bytes=VMEM_LIMIT_BYTES),
        name="mlstm",
    )(qb, kbt, vb, og, bcs, li, gh)


def _post_kernel(x_ref, ya_ref, yb_ref, sga_ref, sgb_ref, wfo_ref, wmo_ref, wo_ref, gffn_ref,
                 wg_ref, wu_ref, wd_ref, o_ref):
    pa = jnp.dot(ya_ref[...], wfo_ref[...], preferred_element_type=F32)
    pb = jnp.dot(yb_ref[...], wmo_ref[...], preferred_element_type=F32)
    merged = sga_ref[...].astype(F32) * pa + sgb_ref[...].astype(F32) * pb
    x1 = x_ref[...] + jnp.dot(merged.astype(BF16), wo_ref[...], preferred_element_type=F32)
    h2 = (x1 * lax.rsqrt(jnp.mean(x1 * x1, axis=-1, keepdims=True) + EPS) * gffn_ref[...]).astype(BF16)
    gate = jnp.dot(h2, wg_ref[...], preferred_element_type=F32)
    up = jnp.dot(h2, wu_ref[...], preferred_element_type=F32)
    act = (gate * _sigmoid(gate) * up).astype(BF16)
    o_ref[...] = x1 + jnp.dot(act, wd_ref[...], preferred_element_type=F32)


def _post_call(x, ya, yb, sga, sgb, wfo, wmo, wo, gffn, wg, wu, wd, *, tm):
    B, S, D = x.shape
    tok = lambda width: pl.BlockSpec((None, tm, width), lambda b, i: (b, i, 0))
    in_specs = [tok(D), tok(ya.shape[2]), tok(yb.shape[2]), tok(D), tok(D),
                _const_spec(wfo.shape), _const_spec(wmo.shape), _const_spec(wo.shape), _const_spec(gffn.shape),
                _const_spec(wg.shape), _const_spec(wu.shape), _const_spec(wd.shape)]
    return pl.pallas_call(
        _post_kernel, grid=(B, S // tm), in_specs=in_specs, out_specs=tok(D),
        out_shape=jax.ShapeDtypeStruct((B, S, D), x.dtype),
        compiler_params=pltpu.CompilerParams(
            dimension_semantics=("arbitrary", "arbitrary"), vmem_limit_bytes=VMEM_LIMIT_BYTES),
        name="post",
    )(x, ya, yb, sga, sgb, wfo, wmo, wo, gffn, wg, wu, wd)


def _block(x, g_mix, w_in, b_fox_f, g_q_fox, g_k_fox, conv_w, conv_b, b_mlstm_i, b_mlstm_f,
           g_mlstm_h, w_fox_out, w_mlstm_out, w_o, g_ffn, w_gate, w_up, w_down):
    B, S, D = x.shape
    fw = w_fox_out.shape[0]
    mw = w_mlstm_out.shape[0]
    fh = fw // FOX_HEAD_DIM
    assert fh == 8 and mw // LANES == MLSTM_HEADS and fw % LANES == 0
    tm = min(512, S)
    t_attn = min(512, S)
    chunk = min(128, S)
    assert S % tm == 0 and S % t_attn == 0 and S % chunk == 0

    sizes = (fw, fw, fw, fh, mw, mw, mw, MLSTM_HEADS, MLSTM_HEADS, mw, D, D)
    offs = [0]
    for sz in sizes:
        offs.append(offs[-1] + sz)
    col = lambda k: w_in[:, offs[k]:offs[k + 1]]
    w_main = jnp.concatenate([col(0), col(1), col(2), col(4), col(5), col(6), col(9), col(10), col(11)],
                             axis=1).astype(BF16)
    w_gt = jnp.concatenate([col(3), col(7), col(8)], axis=1).T.astype(BF16)
    gate_bias = jnp.concatenate([b_fox_f, b_mlstm_i, b_mlstm_f]).astype(F32)[:, None]
    gq = jnp.tile(g_q_fox.astype(F32), fh)[None, :]
    gk = jnp.tile(g_k_fox.astype(F32), fh)[None, :]

    (qa, ka, va, qb, kbt, vb, og, sga, sgb, gates) = _proj_call(
        x, g_mix.astype(F32)[None, :], w_main, w_gt, gate_bias, gq, gk,
        conv_w.astype(F32), conv_b.astype(F32)[None, :], tm=tm)

    scans = _scan_call(gates, chunk=chunk)
    c5 = scans[:, :fh].reshape(B, fh // 2, 2, S // t_attn, t_attn)
    li = scans[:, fh:fh + MLSTM_HEADS].reshape(B, MLSTM_HEADS, S // chunk, chunk)
    bcs = scans[:, fh + MLSTM_HEADS:].reshape(B, MLSTM_HEADS, S // chunk, chunk)

    qk_bound = (FOX_HEAD_DIM * (FOX_HEAD_DIM ** -0.5 * LOG2E) * 1.02
                * jnp.max(jnp.abs(g_q_fox.astype(F32))) * jnp.max(jnp.abs(g_k_fox.astype(F32))))
    cstart = jnp.concatenate([c5[..., 0].reshape(-1), qk_bound[None]])
    ya = _fox_call(qa, ka, va, c5, cstart, t=t_attn)
    nchunk = 2 if S % (2 * chunk) == 0 else 1
    yb = _mlstm_call(qb, kbt, vb, og, bcs, li, g_mlstm_h.astype(F32)[None, :], L=chunk, nchunk=nchunk)

    return _post_call(x, ya, yb, sga, sgb, w_fox_out.astype(BF16), w_mlstm_out.astype(BF16),
                      w_o.astype(BF16), g_ffn.astype(F32)[None, :], w_gate.astype(BF16),
                      w_up.astype(BF16), w_down.astype(BF16), tm=tm)


def kernel(x, g_mix, w_in, b_fox_f, g_q_fox, g_k_fox, conv_w, conv_b, b_mlstm_i, b_mlstm_f, g_mlstm_h,
           w_fox_out, w_mlstm_out, w_o, g_ffn, w_gate, w_up, w_down):
    for l in range(g_mix.shape[0]):
        x = _block(x, g_mix[l], w_in[l], b_fox_f[l], g_q_fox[l], g_k_fox[l], conv_w[l], conv_b[l],
                   b_mlstm_i[l], b_mlstm_f[l], g_mlstm_h[l], w_fox_out[l], w_mlstm_out[l], w_o[l],
                   g_ffn[l], w_gate[l], w_up[l], w_down[l])
    return x
```

```python
import functools

import jax
import jax.numpy as jnp
from jax import lax
from jax.experimental import pallas as pl
from jax.experimental.pallas import tpu as pltpu

EPS = 1e-6
FOX_HEAD_DIM = 64
MLSTM_HEADS = 4
CONV_WIDTH = 4
LANES = 128
NEG_BIG = -1e30
LOG2E = 1.4426950408889634
SAFE_EXP2_RANGE = 100.0
VMEM_LIMIT_BYTES = 56 * 1024 * 1024

F32 = jnp.float32
BF16 = jnp.bfloat16


def _const_spec(shape):
    nd = len(shape)
    return pl.BlockSpec(shape, lambda *_: (0,) * nd, pipeline_mode=pl.Buffered(1))


def _log_sigmoid(z):
    return jnp.minimum(z, 0.0) - jnp.log(1.0 + jnp.exp(-jnp.abs(z)))


def _sigmoid(z):
    return 1.0 / (1.0 + jnp.exp2(z * (-LOG2E)))


def _head_rms(x, g_row, lo_mask):
    outs = []
    for j in range(x.shape[1] // LANES):
        v = x[:, j * LANES:(j + 1) * LANES]
        v2 = v * v
        s_lo = jnp.sum(jnp.where(lo_mask, v2, 0.0), axis=-1, keepdims=True)
        s_hi = jnp.sum(jnp.where(lo_mask, 0.0, v2), axis=-1, keepdims=True)
        ms = jnp.where(lo_mask, s_lo, s_hi) * (1.0 / FOX_HEAD_DIM)
        outs.append(v * lax.rsqrt(ms + EPS))
    return jnp.concatenate(outs, axis=-1) * g_row


def _proj_kernel(x_ref, gmix_ref, w_ref, wg_ref, gb_ref, gq_ref, gk_ref, cw_ref, cb_ref,
                 qa_ref, ka_ref, va_ref, qb_ref, kbt_ref, vb_ref, og_ref, sga_ref, sgb_ref, gates_ref,
                 cbuf_ref, *, tm, nsub, fw, mw, d_model, k_scale):
    i = pl.program_id(1)
    lane = lax.broadcasted_iota(jnp.int32, (1, LANES), 1)
    lo_mask = lane < FOX_HEAD_DIM

    @pl.when(i == 0)
    def _():
        cbuf_ref[0:8, :] = jnp.zeros((8, 2 * mw), F32)

    ts = tm // nsub
    for r in range(nsub):
        rows = slice(r * ts, (r + 1) * ts)
        c0 = 8 + r * ts
        x = x_ref[rows, :]
        h = x * lax.rsqrt(jnp.mean(x * x, axis=-1, keepdims=True) + EPS) * gmix_ref[...]
        hb = h.astype(BF16)

        def proj(lo, width):
            return jnp.dot(hb, w_ref[:, lo:lo + width], preferred_element_type=F32)

        off = 0
        qa_ref[rows, :] = (_head_rms(proj(off, fw), gq_ref[...], lo_mask)
                           * (FOX_HEAD_DIM ** -0.5 * LOG2E)).astype(BF16)
        off += fw
        ka_ref[rows, :] = _head_rms(proj(off, fw), gk_ref[...], lo_mask).astype(BF16)
        off += fw
        va_ref[rows, :] = proj(off, fw).astype(BF16)
        off += fw
        cbuf_ref[c0:c0 + ts, 0:mw] = proj(off, mw)
        off += mw
        cbuf_ref[c0:c0 + ts, mw:2 * mw] = proj(off, mw)
        off += mw
        vb_ref[rows, :] = proj(off, mw).astype(BF16)
        off += mw
        og_ref[rows, :] = _sigmoid(proj(off, mw)).astype(BF16)
        off += mw
        sga_ref[rows, :] = _sigmoid(proj(off, d_model)).astype(BF16)
        off += d_model
        sgb_ref[rows, :] = _sigmoid(proj(off, d_model)).astype(BF16)

        z = lax.dot_general(wg_ref[...], hb, (((1,), (1,)), ((), ())), preferred_element_type=F32) + gb_ref[...]
        row = lax.broadcasted_iota(jnp.int32, z.shape, 0)
        is_input_gate = (row >= 8) & (row < 12)
        gates_ref[:, rows] = jnp.where(is_input_gate, z, _log_sigmoid(z))

        for half in range(2):
            cols = slice(half * mw, (half + 1) * mw)
            y = cb_ref[:, cols] + cw_ref[CONV_WIDTH - 1:CONV_WIDTH, cols] * cbuf_ref[c0:c0 + ts, cols]
            for j in range(CONV_WIDTH - 1):
                r0 = c0 - (CONV_WIDTH - 1) + j
                y = y + cw_ref[j:j + 1, cols] * cbuf_ref[r0:r0 + ts, cols]
            y = y * _sigmoid(y)
            if half == 0:
                qb_ref[rows, :] = y.astype(BF16)
            else:
                kbt_ref[:, rows] = (y * k_scale).T.astype(BF16)
    cbuf_ref[0:8, :] = cbuf_ref[tm:tm + 8, :]


def _proj_call(x, gmix, w_main, w_gt, gate_bias, gq, gk, conv_w, conv_b, *, tm, nsub):
    B, S, D = x.shape
    fw = gq.shape[1]
    mw = conv_w.shape[1] // 2
    n_main = w_main.shape[1]
    grid = (B, S // tm)
    tok = lambda width: pl.BlockSpec((None, tm, width), lambda b, i: (b, i, 0))
    out_shape = (
        jax.ShapeDtypeStruct((B, S, fw), BF16),
        jax.ShapeDtypeStruct((B, S, fw), BF16),
        jax.ShapeDtypeStruct((B, S, fw), BF16),
        jax.ShapeDtypeStruct((B, S, mw), BF16),
        jax.ShapeDtypeStruct((B, mw, S), BF16),
        jax.ShapeDtypeStruct((B, S, mw), BF16),
        jax.ShapeDtypeStruct((B, S, mw), BF16),
        jax.ShapeDtypeStruct((B, S, D), BF16),
        jax.ShapeDtypeStruct((B, S, D), BF16),
        jax.ShapeDtypeStruct((B, 16, S), F32),
    )
    out_specs = (
        tok(fw), tok(fw), tok(fw), tok(mw),
        pl.BlockSpec((None, mw, tm), lambda b, i: (b, 0, i)),
        tok(mw), tok(mw), tok(D), tok(D),
        pl.BlockSpec((None, 16, tm), lambda b, i: (b, 0, i)),
    )
    in_specs = [
        tok(D),
        _const_spec((1, D)),
        _const_spec((D, n_main)),
        _const_spec((16, D)),
        _const_spec((16, 1)),
        _const_spec((1, fw)),
        _const_spec((1, fw)),
        _const_spec((CONV_WIDTH, 2 * mw)),
        _const_spec((1, 2 * mw)),
    ]
    kern = functools.partial(_proj_kernel, tm=tm, nsub=nsub, fw=fw, mw=mw, d_model=D,
                             k_scale=(mw // MLSTM_HEADS) ** -0.5)
    return pl.pallas_call(
        kern, grid=grid, in_specs=in_specs, out_specs=out_specs, out_shape=out_shape,
        scratch_shapes=[pltpu.VMEM((tm + 8, 2 * mw), F32)],
        compiler_params=pltpu.CompilerParams(
            dimension_semantics=("arbitrary", "arbitrary"), vmem_limit_bytes=VMEM_LIMIT_BYTES),
        name="proj",
    )(x, gmix, w_main, w_gt, gate_bias, gq, gk, conv_w, conv_b)


def _scan_kernel(g_ref, o_ref, *, chunk):
    x = g_ref[...]
    S = x.shape[1]
    lane = lax.broadcasted_iota(jnp.int32, x.shape, 1)
    row = lax.broadcasted_iota(jnp.int32, x.shape, 0)
    glob = x
    sh = 1
    while sh < S:
        glob = glob + jnp.where(lane >= sh, pltpu.roll(glob, sh, axis=1), 0.0)
        sh *= 2
    seg = x
    lane_in = lane & (chunk - 1)
    sh = 1
    while sh < chunk:
        seg = seg + jnp.where(lane_in >= sh, pltpu.roll(seg, sh, axis=1), 0.0)
        sh *= 2
    o_ref[...] = jnp.where(row < 8, glob, jnp.where(row < 12, x, seg))


def _scan_call(gates, *, chunk):
    B, R, S = gates.shape
    spec = pl.BlockSpec((None, R, S), lambda b: (b, 0, 0))
    return pl.pallas_call(
        functools.partial(_scan_kernel, chunk=chunk),
        grid=(B,), in_specs=[spec], out_specs=spec,
        out_shape=jax.ShapeDtypeStruct(gates.shape, F32),
        compiler_params=pltpu.CompilerParams(dimension_semantics=("arbitrary",)),
        name="gate_scan",
    )(gates)


def _split3(x):
    hi = x.astype(BF16).astype(F32)
    r = x - hi
    mid = r.astype(BF16).astype(F32)
    lo = (r - mid).astype(BF16).astype(F32)
    return hi, mid, lo


def _fox_kernel(cs_ref, q_ref, k_ref, v_ref, c_ref, o_ref,
                kaug_ref, vtaug_ref, qt_ref, acc_ref, m_ref, s0_ref, s1_ref, p0_ref, p1_ref, *, t, nk, heads):
    b = pl.program_id(0)
    i = pl.program_id(1)
    cs_base = b * heads * nk
    half = FOX_HEAD_DIM
    pairs = heads // 2

    def cs(h, j):
        return cs_ref[cs_base + h * nk + j]

    bound = cs_ref[pl.num_programs(0) * heads * nk]
    safe = 2.0 * bound < SAFE_EXP2_RANGE

    row = lax.broadcasted_iota(jnp.int32, (LANES, t), 0)
    top = row < half
    lane = lax.broadcasted_iota(jnp.int32, (1, LANES), 1)
    lo_mask = lane < half
    row8 = lax.broadcasted_iota(jnp.int32, (8, t), 0)
    ones_a = jnp.where(row < half + 3, 1.0, 0.0)
    ones_b = jnp.where(row < 3, 1.0, 0.0)
    pad = jnp.zeros((half - 8, t), F32)

    acc_ref[...] = jnp.zeros(acc_ref.shape, F32)

    def causal_mask(s):
        kr = lax.broadcasted_iota(jnp.int32, (t, t), 0)
        qc = lax.broadcasted_iota(jnp.int32, (t, t), 1)
        return jnp.where(kr <= qc, s, NEG_BIG)

    def shift_row(h, j):
        return (cs(h, j) - c_ref[h, pl.ds(i, 1), :]) * LOG2E + bound

    def prep_pair(pp):
        lanes = slice(pp * LANES, (pp + 1) * LANES)
        q_t = q_ref[:, lanes].astype(F32).T
        qs = (jnp.where(top, q_t, ones_a).astype(BF16), jnp.where(top, ones_b, q_t).astype(BF16))
        pieces = []
        for e in range(2):
            h = 2 * pp + e
            qt_ref[h] = qs[e]
            brel = (cs(h, i) - c_ref[h, pl.ds(i, 1), :]) * LOG2E
            hi, mid, lo = _split3(brel)
            pieces.append(jnp.where(row8 == 0, hi, jnp.where(row8 == 1, mid, jnp.where(row8 == 2, lo, 0.0))))
        extra_t = jnp.concatenate([pieces[1], pad, pieces[0], pad], axis=0).T.astype(BF16)
        k2 = k_ref[:, lanes]
        ks = (jnp.where(lo_mask, k2, extra_t), jnp.where(lo_mask, extra_t, k2))
        v_t = v_ref[:, lanes].astype(F32).T.astype(BF16)
        one = jnp.ones_like(v_t)
        vs = (jnp.where(top, v_t, one), jnp.where(top, one, v_t))
        for e in range(2):
            kaug_ref[2 * pp + e, i] = ks[e]
            vtaug_ref[2 * pp + e, i] = vs[e]
        return qs, ks

    @pl.when(safe)
    def _():
        for pp in range(pairs):
            qs, ks = prep_pair(pp)
            for e in range(2):
                h = 2 * pp + e
                s = causal_mask(jnp.dot(ks[e], qs[e], preferred_element_type=F32))
                p0_ref[h] = jnp.exp2(s - shift_row(h, i)).astype(BF16)

        def pexp(j, p_ref):
            for h in range(heads):
                s = jnp.dot(kaug_ref[h, j], qt_ref[h], preferred_element_type=F32)
                p_ref[h] = jnp.exp2(s - shift_row(h, j)).astype(BF16)

        def pv(j, p_ref):
            for h in range(heads):
                acc_ref[h] += jnp.dot(vtaug_ref[h, j], p_ref[h], preferred_element_type=F32)

        def pair_body(n, carry):
            j = 2 * n
            pexp(j, p1_ref)
            pv(jnp.where(n == 0, i, j - 1), p0_ref)
            pexp(j + 1, p0_ref)
            pv(j, p1_ref)
            return carry

        lax.fori_loop(0, i // 2, pair_body, 0)

        @pl.when(i % 2 == 0)
        def _():
            pv(jnp.where(i == 0, i, i - 1), p0_ref)

        @pl.when(i % 2 == 1)
        def _():
            pexp(i - 1, p1_ref)
            pv(jnp.where(i == 1, i, i - 2), p0_ref)
            pv(i - 1, p1_ref)

    @pl.when(jnp.logical_not(safe))
    def _():
        for pp in range(pairs):
            prep_pair(pp)

        def one_pair(pp, carry):
            def scores(j, s_ref):
                for e in range(2):
                    s_ref[e] = jnp.dot(kaug_ref[2 * pp + e, j], qt_ref[2 * pp + e], preferred_element_type=F32)

            def consume(j, s_ref, masked):
                for e in range(2):
                    h = 2 * pp + e
                    s = s_ref[e]
                    if masked:
                        s = causal_mask(s)
                    d = (cs(h, j) - cs(h, i)) * LOG2E
                    m_old = m_ref[e]
                    m_new = jnp.maximum(m_old, jnp.max(s, axis=0, keepdims=True) - d)
                    alpha = jnp.exp2(m_old - m_new)
                    p = jnp.exp2(s - (m_new + d))
                    acc_ref[h] = alpha * acc_ref[h] + jnp.dot(vtaug_ref[h, j], p.astype(BF16),
                                                             preferred_element_type=F32)
                    m_ref[e] = m_new

            m_ref[...] = jnp.full(m_ref.shape, NEG_BIG, F32)
            scores(0, s0_ref)

            def pair_body(n, c2):
                j = 2 * n
                scores(j + 1, s1_ref)
                consume(j, s0_ref, False)
                scores(j + 2, s0_ref)
                consume(j + 1, s1_ref, False)
                return c2

            lax.fori_loop(0, i // 2, pair_body, 0)

            @pl.when(i % 2 == 0)
            def _():
                consume(i, s0_ref, True)

            @pl.when(i % 2 == 1)
            def _():
                scores(i, s1_ref)
                consume(i - 1, s0_ref, False)
                consume(i, s1_ref, True)

            return carry

        lax.fori_loop(0, pairs, one_pair, 0)

    for pp in range(pairs):
        a0 = acc_ref[2 * pp]
        a1 = acc_ref[2 * pp + 1]
        out_t = jnp.concatenate([a0[:half] / a0[half:], a1[half:] / a1[:half]], axis=0)
        o_ref[:, pp * LANES:(pp + 1) * LANES] = out_t.T.astype(o_ref.dtype)


def _fox_call(qa, ka, va, c4, cstart, *, t):
    B, S, W = qa.shape
    heads = W // FOX_HEAD_DIM
    nk = S // t
    tile = pl.BlockSpec((None, t, W), lambda b, i: (b, i, 0))
    cspec = pl.BlockSpec((None, heads, nk, t), lambda b, i: (b, 0, 0, 0))
    return pl.pallas_call(
        functools.partial(_fox_kernel, t=t, nk=nk, heads=heads),
        grid=(B, nk),
        in_specs=[pl.BlockSpec(memory_space=pltpu.SMEM), tile, tile, tile, cspec],
        out_specs=tile,
        out_shape=jax.ShapeDtypeStruct((B, S, W), BF16),
        scratch_shapes=[pltpu.VMEM((heads, nk, t, LANES), BF16), pltpu.VMEM((heads, nk, LANES, t), BF16),
                        pltpu.VMEM((heads, LANES, t), BF16), pltpu.VMEM((heads, LANES, t), F32),
                        pltpu.VMEM((2, 1, t), F32),
                        pltpu.VMEM((2, t, t), F32), pltpu.VMEM((2, t, t), F32),
                        pltpu.VMEM((heads, t, t), BF16), pltpu.VMEM((heads, t, t), BF16)],
        compiler_params=pltpu.CompilerParams(
            dimension_semantics=("arbitrary", "arbitrary"), vmem_limit_bytes=VMEM_LIMIT_BYTES),
        name="fox_attn",
    )(cstart, qa, ka, va, c4)


def _mlstm_kernel(q_ref, kt_ref, v_ref, og_ref, b_ref, li_ref, gh_ref, o_ref, c_ref, m_ref, *, L, heads, nchunk):
    step = pl.program_id(1)
    d = LANES

    @pl.when(step == 0)
    def _():
        c_ref[...] = jnp.zeros(c_ref.shape, F32)
        m_ref[...] = jnp.zeros(m_ref.shape, F32)

    r = lax.broadcasted_iota(jnp.int32, (L, L), 0)
    c = lax.broadcasted_iota(jnp.int32, (L, L), 1)
    causal = c <= r
    eye = c == r
    ones = jnp.ones((L, d), BF16)
    hs = range(heads)
    col = lambda hh: slice(hh * d, (hh + 1) * d)

    c_state = [c_ref[hh] for hh in hs]
    m_prev = [m_ref[hh] for hh in hs]

    for cc in range(nchunk):
        rows = slice(cc * L, (cc + 1) * L)
        ci = step * nchunk + cc
        q = [q_ref[rows, col(hh)] for hh in hs]
        kt = [kt_ref[col(hh), rows] for hh in hs]
        v_aug = [jnp.concatenate([v_ref[rows, col(hh)], ones], axis=1) for hh in hs]

        s_raw = [jnp.dot(q[hh], kt[hh], preferred_element_type=F32) for hh in hs]
        inter = [jnp.dot(q[hh], c_state[hh].astype(BF16), preferred_element_type=F32) for hh in hs]

        b_row = [b_ref[hh, pl.ds(ci, 1), :] for hh in hs]
        u_row = [li_ref[hh, pl.ds(ci, 1), :] - b_row[hh] for hh in hs]
        b_last = [b_row[hh][:, L - 1:L] for hh in hs]
        u_max = [jnp.max(u_row[hh], axis=1, keepdims=True) for hh in hs]
        m_loc = [b_last[hh] + u_max[hh] for hh in hs]
        kw = [(kt[hh].astype(F32) * jnp.exp(u_row[hh] - u_max[hh])).astype(BF16) for hh in hs]
        upd = [jnp.dot(kw[hh], v_aug[hh], preferred_element_type=F32) for hh in hs]

        b_col = [jnp.sum(jnp.where(eye, b_row[hh], 0.0), axis=1, keepdims=True) for hh in hs]
        dmat = [jnp.where(causal, b_col[hh] + u_row[hh], NEG_BIG) for hh in hs]
        m_d = [jnp.max(dmat[hh], axis=1, keepdims=True) for hh in hs]
        s = [(s_raw[hh] * jnp.exp(dmat[hh] - m_d[hh])).astype(BF16) for hh in hs]
        intra = [jnp.dot(s[hh], v_aug[hh], preferred_element_type=F32) for hh in hs]

        for hh in hs:
            a = b_col[hh] + m_prev[hh]
            m_t = jnp.maximum(a, m_d[hh])
            w_inter = jnp.exp(a - m_t)
            w_intra = jnp.exp(m_d[hh] - m_t)
            num = w_inter * inter[hh][:, :d] + w_intra * intra[hh][:, :d]
            den = w_inter * inter[hh][:, d:] + w_intra * intra[hh][:, d:]
            hval = num / jnp.maximum(jnp.abs(den), jnp.exp(-m_t))

            m_new = jnp.maximum(b_last[hh] + m_prev[hh], m_loc[hh])
            decay = jnp.exp(b_last[hh] + m_prev[hh] - m_new)
            gain = jnp.exp(m_loc[hh] - m_new)
            c_state[hh] = jnp.tile(decay, (1, 2)) * c_state[hh] + jnp.tile(gain, (1, 2)) * upd[hh]
            m_prev[hh] = m_new

            mu = jnp.mean(hval, axis=-1, keepdims=True)
            xc = hval - mu
            yn = xc * lax.rsqrt(jnp.mean(xc * xc, axis=-1, keepdims=True) + EPS)
            o_ref[rows, col(hh)] = (yn * gh_ref[:, col(hh)]
                                    * og_ref[rows, col(hh)].astype(F32)).astype(o_ref.dtype)

    for hh in hs:
        c_ref[hh] = c_state[hh]
        m_ref[hh] = m_prev[hh]


def _mlstm_call(qb, kbt, vb, og, bcs, li, gh, *, L, nchunk):
    B, S, W = qb.shape
    heads = W // LANES
    T = L * nchunk
    grid = (B, S // T)
    tok = pl.BlockSpec((None, T, W), lambda b, ci: (b, ci, 0))
    ktspec = pl.BlockSpec((None, W, T), lambda b, ci: (b, 0, ci))
    gspec = pl.BlockSpec((None, heads, S // L, L), lambda b, ci: (b, 0, 0, 0))
    return pl.pallas_call(
        functools.partial(_mlstm_kernel, L=L, heads=heads, nchunk=nchunk),
        grid=grid, in_specs=[tok, ktspec, tok, tok, gspec, gspec, _const_spec((1, W))], out_specs=tok,
        out_shape=jax.ShapeDtypeStruct((B, S, W), BF16),
        scratch_shapes=[pltpu.VMEM((heads, LANES, 2 * LANES), F32), pltpu.VMEM((heads, 1, LANES), F32)],
        compiler_params=pltpu.CompilerParams(
            dimension_semantics=("arbitrary", "arbitrary"), vmem_limit_bytes=VMEM_LIMIT_BYTES),
        name="mlstm",
    )(qb, kbt, vb, og, bcs, li, gh)


def _post_kernel(x_ref, ya_ref, yb_ref, sga_ref, sgb_ref, wfo_ref, wmo_ref, wo_ref, gffn_ref,
                 wg_ref, wu_ref, wd_ref, o_ref):
    pa = jnp.dot(ya_ref[...], wfo_ref[...], preferred_element_type=F32)
    pb = jnp.dot(yb_ref[...], wmo_ref[...], preferred_element_type=F32)
    merged = sga_ref[...].astype(F32) * pa + sgb_ref[...].astype(F32) * pb
    x1 = x_ref[...] + jnp.dot(merged.astype(BF16), wo_ref[...], preferred_element_type=F32)
    h2 = (x1 * lax.rsqrt(jnp.mean(x1 * x1, axis=-1, keepdims=True) + EPS) * gffn_ref[...]).astype(BF16)
    gate = jnp.dot(h2, wg_ref[...], preferred_element_type=F32)
    up = jnp.dot(h2, wu_ref[...], preferred_element_type=F32)
    act = (gate * _sigmoid(gate) * up).astype(BF16)
    o_ref[...] = x1 + jnp.dot(act, wd_ref[...], preferred_element_type=F32)


def _post_call(x, ya, yb, sga, sgb, wfo, wmo, wo, gffn, wg, wu, wd, *, tm):
    B, S, D = x.shape
    tok = lambda width: pl.BlockSpec((None, tm, width), lambda b, i: (b, i, 0))
    in_specs = [tok(D), tok(ya.shape[2]), tok(yb.shape[2]), tok(D), tok(D),
                _const_spec(wfo.shape), _const_spec(wmo.shape), _const_spec(wo.shape), _const_spec(gffn.shape),
                _const_spec(wg.shape), _const_spec(wu.shape), _const_spec(wd.shape)]
    return pl.pallas_call(
        _post_kernel, grid=(B, S // tm), in_specs=in_specs, out_specs=tok(D),
        out_shape=jax.ShapeDtypeStruct((B, S, D), x.dtype),
        compiler_params=pltpu.CompilerParams(
            dimension_semantics=("arbitrary", "arbitrary"), vmem_limit_bytes=VMEM_LIMIT_BYTES),
        name="post",
    )(x, ya, yb, sga, sgb, wfo, wmo, wo, gffn, wg, wu, wd)


def _block(x, g_mix, w_in, b_fox_f, g_q_fox, g_k_fox, conv_w, conv_b, b_mlstm_i, b_mlstm_f,
           g_mlstm_h, w_fox_out, w_mlstm_out, w_o, g_ffn, w_gate, w_up, w_down):
    B, S, D = x.shape
    fw = w_fox_out.shape[0]
    mw = w_mlstm_out.shape[0]
    fh = fw // FOX_HEAD_DIM
    assert fh == 8 and mw // LANES == MLSTM_HEADS and fw % LANES == 0
    tm = min(512, S)
    t_attn = min(512, S)
    chunk = min(128, S)
    assert S % tm == 0 and S % t_attn == 0 and S % chunk == 0

    sizes = (fw, fw, fw, fh, mw, mw, mw, MLSTM_HEADS, MLSTM_HEADS, mw, D, D)
    offs = [0]
    for sz in sizes:
        offs.append(offs[-1] + sz)
    col = lambda k: w_in[:, offs[k]:offs[k + 1]]
    w_main = jnp.concatenate([col(0), col(1), col(2), col(4), col(5), col(6), col(9), col(10), col(11)],
                             axis=1).astype(BF16)
    w_gt = jnp.concatenate([col(3), col(7), col(8)], axis=1).T.astype(BF16)
    gate_bias = jnp.concatenate([b_fox_f, b_mlstm_i, b_mlstm_f]).astype(F32)[:, None]
    gq = jnp.tile(g_q_fox.astype(F32), fh)[None, :]
    gk = jnp.tile(g_k_fox.astype(F32), fh)[None, :]

    (qa, ka, va, qb, kbt, vb, og, sga, sgb, gates) = _proj_call(
        x, g_mix.astype(F32)[None, :], w_main, w_gt, gate_bias, gq, gk,
        conv_w.astype(F32), conv_b.astype(F32)[None, :], tm=tm, nsub=2 if tm % 512 == 0 else 1)

    scans = _scan_call(gates, chunk=chunk)
    c4 = scans[:, :fh].reshape(B, fh, S // t_attn, t_attn)
    li = scans[:, fh:fh + MLSTM_HEADS].reshape(B, MLSTM_HEADS, S // chunk, chunk)
    bcs = scans[:, fh + MLSTM_HEADS:].reshape(B, MLSTM_HEADS, S // chunk, chunk)

    qk_bound = (FOX_HEAD_DIM * (FOX_HEAD_DIM ** -0.5 * LOG2E) * 1.02
                * jnp.max(jnp.abs(g_q_fox.astype(F32))) * jnp.max(jnp.abs(g_k_fox.astype(F32))))
    cstart = jnp.concatenate([c4[..., 0].reshape(-1), qk_bound[None]])
    ya = _fox_call(qa, ka, va, c4, cstart, t=t_attn)
    nchunk = 2 if S % (2 * chunk) == 0 else 1
    yb = _mlstm_call(qb, kbt, vb, og, bcs, li, g_mlstm_h.astype(F32)[None, :], L=chunk, nchunk=nchunk)

    return _post_call(x, ya, yb, sga, sgb, w_fox_out.astype(BF16), w_mlstm_out.astype(BF16),
                      w_o.astype(BF16), g_ffn.astype(F32)[None, :], w_gate.astype(BF16),
                      w_up.astype(BF16), w_down.astype(BF16), tm=tm)


def kernel(x, g_mix, w_in, b_fox_f, g_q_fox, g_k_fox, conv_w, conv_b, b_mlstm_i, b_mlstm_f, g_mlstm_h,
           w_fox_out, w_mlstm_out, w_o, g_ffn, w_gate, w_up, w_down):
    for l in range(g_mix.shape[0]):
        x = _block(x, g_mix[l], w_in[l], b_fox_f[l], g_q_fox[l], g_k_fox[l], conv_w[l], conv_b[l],
                   b_mlstm_i[l], b_mlstm_f[l], g_mlstm_h[l], w_fox_out[l], w_mlstm_out[l], w_o[l],
                   g_ffn[l], w_gate[l], w_up[l], w_down[l])
    return x
```

```python
import functools

import jax
import jax.numpy as jnp
from jax import lax
from jax.experimental import pallas as pl
from jax.experimental.pallas import tpu as pltpu

EPS = 1e-6
FOX_HEAD_DIM = 64
MLSTM_HEADS = 4
CONV_WIDTH = 4
LANES = 128
MXU_TILE = 256
NEG_BIG = -1e30
LOG2E = 1.4426950408889634
SAFE_EXP2_RANGE = 100.0
VMEM_LIMIT_BYTES = 56 * 1024 * 1024
TAIL_VMEM_LIMIT_BYTES = 60 * 1024 * 1024

F32 = jnp.float32
BF16 = jnp.bfloat16


def _const_spec(shape):
    nd = len(shape)
    return pl.BlockSpec(shape, lambda *_: (0,) * nd, pipeline_mode=pl.Buffered(1))


def _exp(x):
    return jnp.exp2(x * LOG2E)


def _log_sigmoid(z):
    return jnp.minimum(z, 0.0) - jnp.log(1.0 + _exp(-jnp.abs(z)))


def _sigmoid(z):
    return 1.0 / (1.0 + jnp.exp2(z * (-LOG2E)))


def _head_rms(x, g_row, lo_mask):
    outs = []
    for j in range(x.shape[1] // LANES):
        v = x[:, j * LANES:(j + 1) * LANES]
        v2 = v * v
        s_lo = jnp.sum(jnp.where(lo_mask, v2, 0.0), axis=-1, keepdims=True)
        s_hi = jnp.sum(jnp.where(lo_mask, 0.0, v2), axis=-1, keepdims=True)
        outs.append(v * lax.rsqrt(jnp.where(lo_mask, s_lo, s_hi) + FOX_HEAD_DIM * EPS))
    return jnp.concatenate(outs, axis=-1) * g_row


def _proj_kernel(x_ref, gmix_ref, w_ref, wg_ref, gb_ref, gq_ref, gk_ref, cw_ref, cb_ref,
                 qa_ref, ka_ref, va_ref, qb_ref, kbt_ref, vb_ref, og_ref, mga_ref, mgb_ref, gates_ref,
                 cbuf_ref, *, tm, nsub, fw, mw, d_model, k_scale):
    i = pl.program_id(1)
    lane = lax.broadcasted_iota(jnp.int32, (1, LANES), 1)
    lo_mask = lane < FOX_HEAD_DIM

    @pl.when(i == 0)
    def _():
        cbuf_ref[0:8, :] = jnp.zeros((8, 2 * mw), F32)

    ts = tm // nsub
    for r in range(nsub):
        rows = slice(r * ts, (r + 1) * ts)
        c0 = 8 + r * ts
        x = x_ref[rows, :]
        h = x * lax.rsqrt(jnp.mean(x * x, axis=-1, keepdims=True) + EPS) * gmix_ref[...]
        hb = h.astype(BF16)

        def proj(lo, width):
            return jnp.dot(hb, w_ref[:, lo:lo + width], preferred_element_type=F32)

        off = 0
        qa_ref[rows, :] = _head_rms(proj(off, fw), gq_ref[...], lo_mask).astype(BF16)
        off += fw
        ka_ref[rows, :] = _head_rms(proj(off, fw), gk_ref[...], lo_mask).astype(BF16)
        off += fw
        va_ref[rows, :] = proj(off, fw).astype(BF16)
        off += fw
        cbuf_ref[c0:c0 + ts, 0:mw] = proj(off, mw)
        off += mw
        cbuf_ref[c0:c0 + ts, mw:2 * mw] = proj(off, mw)
        off += mw
        vb_ref[rows, :] = proj(off, mw).astype(BF16)
        off += mw
        og_ref[rows, :] = _sigmoid(proj(off, mw)).astype(BF16)
        off += mw
        mga_ref[rows, :] = _sigmoid(proj(off, d_model)).astype(BF16)
        off += d_model
        mgb_ref[rows, :] = _sigmoid(proj(off, d_model)).astype(BF16)

        z = lax.dot_general(wg_ref[...], hb, (((1,), (1,)), ((), ())), preferred_element_type=F32) + gb_ref[...]
        row = lax.broadcasted_iota(jnp.int32, z.shape, 0)
        is_input_gate = (row >= 8) & (row < 12)
        gates_ref[:, rows] = jnp.where(is_input_gate, z, _log_sigmoid(z))

        for half in range(2):
            cols = slice(half * mw, (half + 1) * mw)
            y = cb_ref[:, cols] + cw_ref[CONV_WIDTH - 1:CONV_WIDTH, cols] * cbuf_ref[c0:c0 + ts, cols]
            for j in range(CONV_WIDTH - 1):
                r0 = c0 - (CONV_WIDTH - 1) + j
                y = y + cw_ref[j:j + 1, cols] * cbuf_ref[r0:r0 + ts, cols]
            y = y * _sigmoid(y)
            if half == 0:
                qb_ref[rows, :] = y.astype(BF16)
            else:
                kbt_ref[:, rows] = (y * k_scale).T.astype(BF16)
    cbuf_ref[0:8, :] = cbuf_ref[tm:tm + 8, :]


def _proj_call(x, gmix, w_main, w_gt, gate_bias, gq, gk, conv_w, conv_b, *, tm, nsub):
    B, S, D = x.shape
    fw = gq.shape[1]
    mw = conv_w.shape[1] // 2
    n_main = w_main.shape[1]
    grid = (B, S // tm)
    tok = lambda width: pl.BlockSpec((None, tm, width), lambda b, i: (b, i, 0))
    out_shape = (
        jax.ShapeDtypeStruct((B, S, fw), BF16),
        jax.ShapeDtypeStruct((B, S, fw), BF16),
        jax.ShapeDtypeStruct((B, S, fw), BF16),
        jax.ShapeDtypeStruct((B, S, mw), BF16),
        jax.ShapeDtypeStruct((B, mw, S), BF16),
        jax.ShapeDtypeStruct((B, S, mw), BF16),
        jax.ShapeDtypeStruct((B, S, mw), BF16),
        jax.ShapeDtypeStruct((B, S, D), BF16),
        jax.ShapeDtypeStruct((B, S, D), BF16),
        jax.ShapeDtypeStruct((B, 16, S), F32),
    )
    out_specs = (
        tok(fw), tok(fw), tok(fw), tok(mw),
        pl.BlockSpec((None, mw, tm), lambda b, i: (b, 0, i)),
        tok(mw), tok(mw), tok(D), tok(D),
        pl.BlockSpec((None, 16, tm), lambda b, i: (b, 0, i)),
    )
    in_specs = [
        tok(D),
        _const_spec((1, D)),
        _const_spec((D, n_main)),
        _const_spec((16, D)),
        _const_spec((16, 1)),
        _const_spec((1, fw)),
        _const_spec((1, fw)),
        _const_spec((CONV_WIDTH, 2 * mw)),
        _const_spec((1, 2 * mw)),
    ]
    kern = functools.partial(_proj_kernel, tm=tm, nsub=nsub, fw=fw, mw=mw, d_model=D,
                             k_scale=(mw // MLSTM_HEADS) ** -0.5)
    return pl.pallas_call(
        kern, grid=grid, in_specs=in_specs, out_specs=out_specs, out_shape=out_shape,
        scratch_shapes=[pltpu.VMEM((tm + 8, 2 * mw), F32)],
        compiler_params=pltpu.CompilerParams(
            dimension_semantics=("arbitrary", "arbitrary"), vmem_limit_bytes=VMEM_LIMIT_BYTES),
        name="proj",
    )(x, gmix, w_main, w_gt, gate_bias, gq, gk, conv_w, conv_b)


def _scan_kernel(g_ref, o_ref, *, chunk):
    x = g_ref[...]
    S = x.shape[1]
    lane = lax.broadcasted_iota(jnp.int32, x.shape, 1)
    row = lax.broadcasted_iota(jnp.int32, x.shape, 0)
    glob = x
    sh = 1
    while sh < S:
        glob = glob + jnp.where(lane >= sh, pltpu.roll(glob, sh, axis=1), 0.0)
        sh *= 2
    seg = x
    lane_in = lane & (chunk - 1)
    sh = 1
    while sh < chunk:
        seg = seg + jnp.where(lane_in >= sh, pltpu.roll(seg, sh, axis=1), 0.0)
        sh *= 2
    o_ref[...] = jnp.where(row < 8, glob, jnp.where(row < 12, x, seg))


def _scan_call(gates, *, chunk):
    B, R, S = gates.shape
    spec = pl.BlockSpec((None, R, S), lambda b: (b, 0, 0))
    return pl.pallas_call(
        functools.partial(_scan_kernel, chunk=chunk),
        grid=(B,), in_specs=[spec], out_specs=spec,
        out_shape=jax.ShapeDtypeStruct(gates.shape, F32),
        compiler_params=pltpu.CompilerParams(dimension_semantics=("arbitrary",)),
        name="gate_scan",
    )(gates)


def _split3(x):
    hi = x.astype(BF16).astype(F32)
    r = x - hi
    mid = r.astype(BF16).astype(F32)
    lo = (r - mid).astype(BF16).astype(F32)
    return hi, mid, lo


def _fox_kernel(cs_ref, q_ref, k_ref, v_ref, c_ref, o_ref,
                kaug_ref, vtaug_ref, qt_ref, acc_ref, m_ref, s0_ref, s1_ref, p0_ref, p1_ref, *, t, nk, heads):
    b = pl.program_id(0)
    i = pl.program_id(1)
    cs_base = b * heads * nk
    half = FOX_HEAD_DIM
    pairs = heads // 2

    def cs(h, j):
        return cs_ref[cs_base + h * nk + j]

    bound = cs_ref[pl.num_programs(0) * heads * nk]
    safe = 2.0 * bound < SAFE_EXP2_RANGE

    row = lax.broadcasted_iota(jnp.int32, (LANES, t), 0)
    top = row < half
    lane = lax.broadcasted_iota(jnp.int32, (1, LANES), 1)
    lo_mask = lane < half
    row8 = lax.broadcasted_iota(jnp.int32, (8, t), 0)
    ones_a = jnp.where(row < half + 3, 1.0, 0.0)
    ones_b = jnp.where(row < 3, 1.0, 0.0)
    pad = jnp.zeros((half - 8, t), F32)

    acc_ref[...] = jnp.zeros(acc_ref.shape, F32)

    def causal_mask(s):
        kr = lax.broadcasted_iota(jnp.int32, (t, t), 0)
        qc = lax.broadcasted_iota(jnp.int32, (t, t), 1)
        return jnp.where(kr <= qc, s, NEG_BIG)

    def shift_row(h, j):
        return (cs(h, j) - c_ref[h, pl.ds(i, 1), :]) * LOG2E + bound

    def prep_pair(pp):
        lanes = slice(pp * LANES, (pp + 1) * LANES)
        q_t = q_ref[:, lanes].astype(F32).T
        qs = (jnp.where(top, q_t, ones_a).astype(BF16), jnp.where(top, ones_b, q_t).astype(BF16))
        pieces = []
        for e in range(2):
            h = 2 * pp + e
            qt_ref[h] = qs[e]
            brel = (cs(h, i) - c_ref[h, pl.ds(i, 1), :]) * LOG2E
            hi, mid, lo = _split3(brel)
            pieces.append(jnp.where(row8 == 0, hi, jnp.where(row8 == 1, mid, jnp.where(row8 == 2, lo, 0.0))))
        extra_t = jnp.concatenate([pieces[1], pad, pieces[0], pad], axis=0).T.astype(BF16)
        k2 = k_ref[:, lanes]
        ks = (jnp.where(lo_mask, k2, extra_t), jnp.where(lo_mask, extra_t, k2))
        v_t = v_ref[:, lanes].astype(F32).T.astype(BF16)
        one = jnp.ones_like(v_t)
        vs = (jnp.where(top, v_t, one), jnp.where(top, one, v_t))
        for e in range(2):
            kaug_ref[2 * pp + e, i] = ks[e]
            vtaug_ref[2 * pp + e, i] = vs[e]
        return qs, ks

    @pl.when(safe)
    def _():
        for pp in range(pairs):
            qs, ks = prep_pair(pp)
            for e in range(2):
                h = 2 * pp + e
                s = causal_mask(jnp.dot(ks[e], qs[e], preferred_element_type=F32))
                p0_ref[h] = jnp.exp2(s - shift_row(h, i)).astype(BF16)

        def pexp(j, p_ref):
            for h in range(heads):
                s = jnp.dot(kaug_ref[h, j], qt_ref[h], preferred_element_type=F32)
                p_ref[h] = jnp.exp2(s - shift_row(h, j)).astype(BF16)

        def pv(j, p_ref):
            for h in range(heads):
                acc_ref[h] += jnp.dot(vtaug_ref[h, j], p_ref[h], preferred_element_type=F32)

        def pair_body(n, carry):
            j = 2 * n
            pexp(j, p1_ref)
            pv(jnp.where(n == 0, i, j - 1), p0_ref)
            pexp(j + 1, p0_ref)
            pv(j, p1_ref)
            return carry

        lax.fori_loop(0, i // 2, pair_body, 0)

        @pl.when(i % 2 == 0)
        def _():
            pv(jnp.where(i == 0, i, i - 1), p0_ref)

        @pl.when(i % 2 == 1)
        def _():
            pexp(i - 1, p1_ref)
            pv(jnp.where(i == 1, i, i - 2), p0_ref)
            pv(i - 1, p1_ref)

    @pl.when(jnp.logical_not(safe))
    def _():
        for pp in range(pairs):
            prep_pair(pp)

        def one_pair(pp, carry):
            def scores(j, s_ref):
                for e in range(2):
                    s_ref[e] = jnp.dot(kaug_ref[2 * pp + e, j], qt_ref[2 * pp + e], preferred_element_type=F32)

            def consume(j, s_ref, masked):
                for e in range(2):
                    h = 2 * pp + e
                    s = s_ref[e]
                    if masked:
                        s = causal_mask(s)
                    d = (cs(h, j) - cs(h, i)) * LOG2E
                    m_old = m_ref[e]
                    m_new = jnp.maximum(m_old, jnp.max(s, axis=0, keepdims=True) - d)
                    alpha = jnp.exp2(m_old - m_new)
                    p = jnp.exp2(s - (m_new + d))
                    acc_ref[h] = alpha * acc_ref[h] + jnp.dot(vtaug_ref[h, j], p.astype(BF16),
                                                             preferred_element_type=F32)
                    m_ref[e] = m_new

            m_ref[...] = jnp.full(m_ref.shape, NEG_BIG, F32)
            scores(0, s0_ref)

            def pair_body(n, c2):
                j = 2 * n
                scores(j + 1, s1_ref)
                consume(j, s0_ref, False)
                scores(j + 2, s0_ref)
                consume(j + 1, s1_ref, False)
                return c2

            lax.fori_loop(0, i // 2, pair_body, 0)

            @pl.when(i % 2 == 0)
            def _():
                consume(i, s0_ref, True)

            @pl.when(i % 2 == 1)
            def _():
                scores(i, s1_ref)
                consume(i - 1, s0_ref, False)
                consume(i, s1_ref, True)

            return carry

        lax.fori_loop(0, pairs, one_pair, 0)

    for pp in range(pairs):
        a0 = acc_ref[2 * pp]
        a1 = acc_ref[2 * pp + 1]
        out_t = jnp.concatenate([a0[:half] / a0[half:], a1[half:] / a1[:half]], axis=0)
        o_ref[:, pp * LANES:(pp + 1) * LANES] = out_t.T.astype(o_ref.dtype)


def _fox_call(qa, ka, va, c4, cstart, *, t):
    B, S, W = qa.shape
    heads = W // FOX_HEAD_DIM
    nk = S // t
    tile = pl.BlockSpec((None, t, W), lambda b, i: (b, i, 0))
    cspec = pl.BlockSpec((None, heads, nk, t), lambda b, i: (b, 0, 0, 0))
    return pl.pallas_call(
        functools.partial(_fox_kernel, t=t, nk=nk, heads=heads),
        grid=(B, nk),
        in_specs=[pl.BlockSpec(memory_space=pltpu.SMEM), tile, tile, tile, cspec],
        out_specs=tile,
        out_shape=jax.ShapeDtypeStruct((B, S, W), BF16),
        scratch_shapes=[pltpu.VMEM((heads, nk, t, LANES), BF16), pltpu.VMEM((heads, nk, LANES, t), BF16),
                        pltpu.VMEM((heads, LANES, t), BF16), pltpu.VMEM((heads, LANES, t), F32),
                        pltpu.VMEM((2, 1, t), F32),
                        pltpu.VMEM((2, t, t), F32), pltpu.VMEM((2, t, t), F32),
                        pltpu.VMEM((heads, t, t), BF16), pltpu.VMEM((heads, t, t), BF16)],
        compiler_params=pltpu.CompilerParams(
            dimension_semantics=("arbitrary", "arbitrary"), vmem_limit_bytes=VMEM_LIMIT_BYTES),
        name="fox_attn",
    )(cstart, qa, ka, va, c4)


def _mlstm_stages(q_ref, kt_ref, v_ref, og_ref, b_ref, li_ref, gh_ref, c_ref, m_ref, out_ref, tile, *,
                  L, heads, nchunk):
    d = LANES
    r = lax.broadcasted_iota(jnp.int32, (L, L), 0)
    c = lax.broadcasted_iota(jnp.int32, (L, L), 1)
    causal = c <= r
    eye = c == r
    ones = jnp.ones((L, d), BF16)
    items = [(cc, hh) for cc in range(nchunk) for hh in range(heads)]
    col = lambda hh: slice(hh * d, (hh + 1) * d)
    rows = lambda cc: slice(cc * L, (cc + 1) * L)
    q_of = lambda it: q_ref[rows(it[0]), col(it[1])]
    kt_of = lambda it: kt_ref[col(it[1]), rows(it[0])]
    v_aug_of = lambda it: jnp.concatenate([v_ref[rows(it[0]), col(it[1])], ones], axis=1)
    st = {}

    def stage_scores():
        st["s_raw"] = {it: jnp.dot(q_of(it), kt_of(it), preferred_element_type=F32) for it in items}

    def stage_updates():
        b_row, u_row = {}, {}
        for cc, hh in items:
            ci = tile * nchunk + cc
            b_row[cc, hh] = b_ref[hh, pl.ds(ci, 1), :]
            u_row[cc, hh] = li_ref[hh, pl.ds(ci, 1), :] - b_row[cc, hh]
        st["b_row"], st["u_row"] = b_row, u_row
        st["b_last"] = {it: b_row[it][:, L - 1:L] for it in items}
        u_max = {it: jnp.max(u_row[it], axis=1, keepdims=True) for it in items}
        st["m_loc"] = {it: st["b_last"][it] + u_max[it] for it in items}
        kw = {it: (kt_of(it).astype(F32) * _exp(u_row[it] - u_max[it])).astype(BF16) for it in items}
        st["upd"] = {it: jnp.dot(kw[it], v_aug_of(it), preferred_element_type=F32) for it in items}

    def stage_intra():
        b_col = {it: jnp.sum(jnp.where(eye, st["b_row"][it], 0.0), axis=1, keepdims=True) for it in items}
        dmat = {it: jnp.where(causal, b_col[it] + st["u_row"][it], NEG_BIG) for it in items}
        st["b_col"] = b_col
        st["m_d"] = {it: jnp.max(dmat[it], axis=1, keepdims=True) for it in items}
        s = {it: (st["s_raw"][it] * _exp(dmat[it] - st["m_d"][it])).astype(BF16) for it in items}
        st["intra"] = {it: jnp.dot(s[it], v_aug_of(it), preferred_element_type=F32) for it in items}

    def stage_inter():
        c_state = [c_ref[hh] for hh in range(heads)]
        m_prev = [m_ref[hh] for hh in range(heads)]
        st["m_in"], c_in = {}, {}
        for cc, hh in items:
            st["m_in"][cc, hh] = m_prev[hh]
            c_in[cc, hh] = c_state[hh]
            m_new = jnp.maximum(st["b_last"][cc, hh] + m_prev[hh], st["m_loc"][cc, hh])
            decay = _exp(st["b_last"][cc, hh] + m_prev[hh] - m_new)
            gain = _exp(st["m_loc"][cc, hh] - m_new)
            c_state[hh] = jnp.tile(decay, (1, 2)) * c_state[hh] + jnp.tile(gain, (1, 2)) * st["upd"][cc, hh]
            m_prev[hh] = m_new
        for hh in range(heads):
            c_ref[hh] = c_state[hh]
            m_ref[hh] = m_prev[hh]
        st["inter"] = {it: jnp.dot(q_of(it), c_in[it].astype(BF16), preferred_element_type=F32)
                       for it in items}

    def stage_output():
        for cc, hh in items:
            a = st["b_col"][cc, hh] + st["m_in"][cc, hh]
            m_t = jnp.maximum(a, st["m_d"][cc, hh])
            w_inter = _exp(a - m_t)
            w_intra = _exp(st["m_d"][cc, hh] - m_t)
            inter, intra = st["inter"][cc, hh], st["intra"][cc, hh]
            num = w_inter * inter[:, :d] + w_intra * intra[:, :d]
            den = w_inter * inter[:, d:] + w_intra * intra[:, d:]
            hval = num / jnp.maximum(jnp.abs(den), _exp(-m_t))
            mu = jnp.mean(hval, axis=-1, keepdims=True)
            xc = hval - mu
            yn = xc * lax.rsqrt(jnp.mean(xc * xc, axis=-1, keepdims=True) + EPS)
            out_ref[rows(cc), col(hh)] = (yn * gh_ref[:, col(hh)]
                                          * og_ref[rows(cc), col(hh)].astype(F32)).astype(out_ref.dtype)

    return [stage_scores, stage_updates, stage_intra, stage_inter, stage_output]


def _tail_kernel(x_ref, ya_ref, mga_ref, mgb_ref, q0_ref, kt0_ref, v0_ref, og0_ref,
                 qn_ref, ktn_ref, vn_ref, ogn_ref, b_ref, li_ref, gh_ref,
                 wfo_ref, wmo_ref, wo_ref, gffn_ref, wg_ref, wu_ref, wd_ref, o_ref,
                 c_ref, m_ref, yb_ref, *, L, heads, nchunk):
    i = pl.program_id(1)
    last = pl.num_programs(1) - 1
    cur = i % 2
    kw = dict(L=L, heads=heads, nchunk=nchunk)

    @pl.when(i == 0)
    def _():
        c_ref[...] = jnp.zeros(c_ref.shape, F32)
        m_ref[...] = jnp.zeros(m_ref.shape, F32)
        for stage in _mlstm_stages(q0_ref, kt0_ref, v0_ref, og0_ref, b_ref, li_ref, gh_ref, c_ref, m_ref,
                                   yb_ref.at[0], 0, **kw):
            stage()

    stages = _mlstm_stages(qn_ref, ktn_ref, vn_ref, ogn_ref, b_ref, li_ref, gh_ref, c_ref, m_ref,
                           yb_ref.at[1 - cur], jnp.minimum(i + 1, last), **kw)

    pa = jnp.dot(ya_ref[...], wfo_ref[...], preferred_element_type=F32)
    stages[0]()
    pb = jnp.dot(yb_ref[cur], wmo_ref[...], preferred_element_type=F32)
    stages[1]()
    merged = mga_ref[...].astype(F32) * pa + mgb_ref[...].astype(F32) * pb
    x1 = x_ref[...] + jnp.dot(merged.astype(BF16), wo_ref[...], preferred_element_type=F32)
    stages[2]()
    h2 = (x1 * lax.rsqrt(jnp.mean(x1 * x1, axis=-1, keepdims=True) + EPS) * gffn_ref[...]).astype(BF16)
    d_ff = wg_ref.shape[1]
    split = (d_ff // 2 + MXU_TILE - 1) // MXU_TILE * MXU_TILE
    acts = []
    for cols, after_gate in ((slice(0, split), stages[3]), (slice(split, d_ff), stages[4])):
        gate = jnp.dot(h2, wg_ref[:, cols], preferred_element_type=F32)
        after_gate()
        up = jnp.dot(h2, wu_ref[:, cols], preferred_element_type=F32)
        acts.append((gate * _sigmoid(gate) * up).astype(BF16))
    act = jnp.concatenate(acts, axis=1)
    o_ref[...] = x1 + jnp.dot(act, wd_ref[...], preferred_element_type=F32)


def _tail_call(x, ya, mga, mgb, qb, kbt, vb, og, bcs, li, gh, wfo, wmo, wo, gffn, wg, wu, wd, *, tm, L):
    B, S, D = x.shape
    W = qb.shape[2]
    heads = W // LANES
    nt = S // tm
    tok = lambda width: pl.BlockSpec((None, tm, width), lambda b, i: (b, i, 0))
    first = pl.BlockSpec((None, tm, W), lambda b, i: (b, 0, 0), pipeline_mode=pl.Buffered(1))
    first_t = pl.BlockSpec((None, W, tm), lambda b, i: (b, 0, 0), pipeline_mode=pl.Buffered(1))
    nxt = pl.BlockSpec((None, tm, W), lambda b, i: (b, jnp.minimum(i + 1, nt - 1), 0))
    nxt_t = pl.BlockSpec((None, W, tm), lambda b, i: (b, 0, jnp.minimum(i + 1, nt - 1)))
    gspec = pl.BlockSpec((None, heads, S // L, L), lambda b, i: (b, 0, 0, 0))
    in_specs = [tok(D), tok(ya.shape[2]), tok(D), tok(D),
                first, first_t, first, first, nxt, nxt_t, nxt, nxt, gspec, gspec, _const_spec(gh.shape),
                _const_spec(wfo.shape), _const_spec(wmo.shape), _const_spec(wo.shape), _const_spec(gffn.shape),
                _const_spec(wg.shape), _const_spec(wu.shape), _const_spec(wd.shape)]
    return pl.pallas_call(
        functools.partial(_tail_kernel, L=L, heads=heads, nchunk=tm // L),
        grid=(B, nt), in_specs=in_specs, out_specs=tok(D),
        out_shape=jax.ShapeDtypeStruct((B, S, D), x.dtype),
        scratch_shapes=[pltpu.VMEM((heads, LANES, 2 * LANES), F32), pltpu.VMEM((heads, 1, LANES), F32),
                        pltpu.VMEM((2, tm, W), BF16)],
        compiler_params=pltpu.CompilerParams(
            dimension_semantics=("arbitrary", "arbitrary"), vmem_limit_bytes=TAIL_VMEM_LIMIT_BYTES),
        name="tail",
    )(x, ya, mga, mgb, qb, kbt, vb, og, qb, kbt, vb, og, bcs, li, gh, wfo, wmo, wo, gffn, wg, wu, wd)


def _block(x, g_mix, w_in, b_fox_f, g_q_fox, g_k_fox, conv_w, conv_b, b_mlstm_i, b_mlstm_f,
           g_mlstm_h, w_fox_out, w_mlstm_out, w_o, g_ffn, w_gate, w_up, w_down):
    B, S, D = x.shape
    fw = w_fox_out.shape[0]
    mw = w_mlstm_out.shape[0]
    fh = fw // FOX_HEAD_DIM
    assert fh == 8 and mw // LANES == MLSTM_HEADS and fw % LANES == 0
    tm = min(512, S)
    t_attn = min(512, S)
    chunk = min(128, S)
    assert S % tm == 0 and S % t_attn == 0 and tm % chunk == 0

    sizes = (fw, fw, fw, fh, mw, mw, mw, MLSTM_HEADS, MLSTM_HEADS, mw, D, D)
    offs = [0]
    for sz in sizes:
        offs.append(offs[-1] + sz)
    col = lambda k: w_in[:, offs[k]:offs[k + 1]]
    w_main = jnp.concatenate([col(0), col(1), col(2), col(4), col(5), col(6), col(9), col(10), col(11)],
                             axis=1).astype(BF16)
    w_gt = jnp.concatenate([col(3), col(7), col(8)], axis=1).T.astype(BF16)
    gate_bias = jnp.concatenate([b_fox_f, b_mlstm_i, b_mlstm_f]).astype(F32)[:, None]
    root_d = FOX_HEAD_DIM ** 0.5
    gq = jnp.tile(g_q_fox.astype(F32) * (root_d * FOX_HEAD_DIM ** -0.5 * LOG2E), fh)[None, :]
    gk = jnp.tile(g_k_fox.astype(F32) * root_d, fh)[None, :]

    (qa, ka, va, qb, kbt, vb, og, mga, mgb, gates) = _proj_call(
        x, g_mix.astype(F32)[None, :], w_main, w_gt, gate_bias, gq, gk,
        conv_w.astype(F32), conv_b.astype(F32)[None, :], tm=tm, nsub=2 if tm % 512 == 0 else 1)

    scans = _scan_call(gates, chunk=chunk)
    c4 = scans[:, :fh].reshape(B, fh, S // t_attn, t_attn)
    li = scans[:, fh:fh + MLSTM_HEADS].reshape(B, MLSTM_HEADS, S // chunk, chunk)
    bcs = scans[:, fh + MLSTM_HEADS:].reshape(B, MLSTM_HEADS, S // chunk, chunk)

    qk_bound = (FOX_HEAD_DIM * (FOX_HEAD_DIM ** -0.5 * LOG2E) * 1.02
                * jnp.max(jnp.abs(g_q_fox.astype(F32))) * jnp.max(jnp.abs(g_k_fox.astype(F32))))
    cstart = jnp.concatenate([c4[..., 0].reshape(-1), qk_bound[None]])
    ya = _fox_call(qa, ka, va, c4, cstart, t=t_attn)
    return _tail_call(x, ya, mga, mgb, qb, kbt, vb, og, bcs, li, g_mlstm_h.astype(F32)[None, :],
                      w_fox_out.astype(BF16), w_mlstm_out.astype(BF16), w_o.astype(BF16),
                      g_ffn.astype(F32)[None, :], w_gate.astype(BF16), w_up.astype(BF16),
                      w_down.astype(BF16), tm=tm, L=chunk)


def kernel(x, g_mix, w_in, b_fox_f, g_q_fox, g_k_fox, conv_w, conv_b, b_mlstm_i, b_mlstm_f, g_mlstm_h,
           w_fox_out, w_mlstm_out, w_o, g_ffn, w_gate, w_up, w_down):
    for l in range(g_mix.shape[0]):
        x = _block(x, g_mix[l], w_in[l], b_fox_f[l], g_q_fox[l], g_k_fox[l], conv_w[l], conv_b[l],
                   b_mlstm_i[l], b_mlstm_f[l], g_mlstm_h[l], w_fox_out[l], w_mlstm_out[l], w_o[l],
                   g_ffn[l], w_gate[l], w_up[l], w_down[l])
    return x
```

```python
import functools

import jax
import jax.numpy as jnp
from jax import lax
from jax.experimental import pallas as pl
from jax.experimental.pallas import tpu as pltpu

EPS = 1e-6
FOX_HEAD_DIM = 64
MLSTM_HEADS = 4
CONV_WIDTH = 4
LANES = 128
MXU_TILE = 256
NEG_BIG = -1e30
LOG2E = 1.4426950408889634
SAFE_EXP2_RANGE = 100.0
VMEM_LIMIT_BYTES = 56 * 1024 * 1024
TAIL_VMEM_LIMIT_BYTES = 60 * 1024 * 1024

F32 = jnp.float32
BF16 = jnp.bfloat16


def _const_spec(shape):
    nd = len(shape)
    return pl.BlockSpec(shape, lambda *_: (0,) * nd, pipeline_mode=pl.Buffered(1))


def _exp(x):
    return jnp.exp2(x * LOG2E)


def _log_sigmoid(z):
    return jnp.minimum(z, 0.0) - jnp.log(1.0 + _exp(-jnp.abs(z)))


def _sigmoid(z):
    return 1.0 / (1.0 + jnp.exp2(z * (-LOG2E)))


def _wsplit_kernel(w_ref, *out_refs, bounds):
    for o_ref, (lo, hi) in zip(out_refs, bounds):
        o_ref[...] = w_ref[:, lo:hi].astype(o_ref.dtype)


def _wsplit_call(w_in, bounds, *, rows):
    D, N = w_in.shape
    widths = [hi - lo for lo, hi in bounds]
    return pl.pallas_call(
        functools.partial(_wsplit_kernel, bounds=tuple(bounds)),
        grid=(D // rows,),
        in_specs=[pl.BlockSpec((rows, N), lambda i: (i, 0))],
        out_specs=tuple(pl.BlockSpec((rows, w), lambda i: (i, 0)) for w in widths),
        out_shape=tuple(jax.ShapeDtypeStruct((D, w), BF16) for w in widths),
        compiler_params=pltpu.CompilerParams(dimension_semantics=("arbitrary",)),
        name="wsplit",
    )(w_in)


def _head_rms(x, g_row, lo_mask):
    outs = []
    for j in range(x.shape[1] // LANES):
        v = x[:, j * LANES:(j + 1) * LANES]
        v2 = v * v
        s_lo = jnp.sum(jnp.where(lo_mask, v2, 0.0), axis=-1, keepdims=True)
        s_hi = jnp.sum(jnp.where(lo_mask, 0.0, v2), axis=-1, keepdims=True)
        outs.append(v * lax.rsqrt(jnp.where(lo_mask, s_lo, s_hi) + FOX_HEAD_DIM * EPS))
    return jnp.concatenate(outs, axis=-1) * g_row


def _proj_kernel(x_ref, gmix_ref, wqa_ref, wka_ref, wva_ref, wqb_ref, wkb_ref, wvb_ref, wog_ref, wga_ref, wgb_ref,
                 wg_ref, gb_ref, gq_ref, gk_ref, cw_ref, cb_ref,
                 qa_ref, ka_ref, va_ref, qb_ref, kbt_ref, vb_ref, og_ref, mga_ref, mgb_ref, gates_ref,
                 cbuf_ref, *, tm, nsub, mw, k_scale):
    i = pl.program_id(1)
    lane = lax.broadcasted_iota(jnp.int32, (1, LANES), 1)
    lo_mask = lane < FOX_HEAD_DIM

    @pl.when(i == 0)
    def _():
        cbuf_ref[0:8, :] = jnp.zeros((8, 2 * mw), F32)

    ts = tm // nsub
    for r in range(nsub):
        rows = slice(r * ts, (r + 1) * ts)
        c0 = 8 + r * ts
        x = x_ref[rows, :]
        h = x * lax.rsqrt(jnp.mean(x * x, axis=-1, keepdims=True) + EPS) * gmix_ref[...]
        hb = h.astype(BF16)

        def proj(w_ref):
            return jnp.dot(hb, w_ref[...], preferred_element_type=F32)

        qa_ref[rows, :] = _head_rms(proj(wqa_ref), gq_ref[...], lo_mask).astype(BF16)
        ka_ref[rows, :] = _head_rms(proj(wka_ref), gk_ref[...], lo_mask).astype(BF16)
        va_ref[rows, :] = proj(wva_ref).astype(BF16)
        cbuf_ref[c0:c0 + ts, 0:mw] = proj(wqb_ref)
        cbuf_ref[c0:c0 + ts, mw:2 * mw] = proj(wkb_ref)
        vb_ref[rows, :] = proj(wvb_ref).astype(BF16)
        og_ref[rows, :] = _sigmoid(proj(wog_ref)).astype(BF16)
        mga_ref[rows, :] = _sigmoid(proj(wga_ref)).astype(BF16)
        mgb_ref[rows, :] = _sigmoid(proj(wgb_ref)).astype(BF16)

        z = lax.dot_general(wg_ref[...], hb, (((1,), (1,)), ((), ())), preferred_element_type=F32) + gb_ref[...]
        row = lax.broadcasted_iota(jnp.int32, z.shape, 0)
        is_input_gate = (row >= 8) & (row < 12)
        gates_ref[:, rows] = jnp.where(is_input_gate, z, _log_sigmoid(z))

        for half in range(2):
            cols = slice(half * mw, (half + 1) * mw)
            y = cb_ref[:, cols] + cw_ref[CONV_WIDTH - 1:CONV_WIDTH, cols] * cbuf_ref[c0:c0 + ts, cols]
            for j in range(CONV_WIDTH - 1):
                r0 = c0 - (CONV_WIDTH - 1) + j
                y = y + cw_ref[j:j + 1, cols] * cbuf_ref[r0:r0 + ts, cols]
            y = y * _sigmoid(y)
            if half == 0:
                qb_ref[rows, :] = y.astype(BF16)
            else:
                kbt_ref[:, rows] = (y * k_scale).T.astype(BF16)
    cbuf_ref[0:8, :] = cbuf_ref[tm:tm + 8, :]


def _proj_call(x, gmix, w_parts, w_gt, gate_bias, gq, gk, conv_w, conv_b, *, tm, nsub):
    B, S, D = x.shape
    fw = gq.shape[1]
    mw = conv_w.shape[1] // 2
    grid = (B, S // tm)
    tok = lambda width: pl.BlockSpec((None, tm, width), lambda b, i: (b, i, 0))
    out_shape = (
        jax.ShapeDtypeStruct((B, S, fw), BF16),
        jax.ShapeDtypeStruct((B, S, fw), BF16),
        jax.ShapeDtypeStruct((B, S, fw), BF16),
        jax.ShapeDtypeStruct((B, S, mw), BF16),
        jax.ShapeDtypeStruct((B, mw, S), BF16),
        jax.ShapeDtypeStruct((B, S, mw), BF16),
        jax.ShapeDtypeStruct((B, S, mw), BF16),
        jax.ShapeDtypeStruct((B, S, D), BF16),
        jax.ShapeDtypeStruct((B, S, D), BF16),
        jax.ShapeDtypeStruct((B, 16, S), F32),
    )
    out_specs = (
        tok(fw), tok(fw), tok(fw), tok(mw),
        pl.BlockSpec((None, mw, tm), lambda b, i: (b, 0, i)),
        tok(mw), tok(mw), tok(D), tok(D),
        pl.BlockSpec((None, 16, tm), lambda b, i: (b, 0, i)),
    )
    in_specs = [
        tok(D),
        _const_spec((1, D)),
        *[_const_spec(w.shape) for w in w_parts],
        _const_spec((16, D)),
        _const_spec((16, 1)),
        _const_spec((1, fw)),
        _const_spec((1, fw)),
        _const_spec((CONV_WIDTH, 2 * mw)),
        _const_spec((1, 2 * mw)),
    ]
    kern = functools.partial(_proj_kernel, tm=tm, nsub=nsub, mw=mw, k_scale=(mw // MLSTM_HEADS) ** -0.5)
    return pl.pallas_call(
        kern, grid=grid, in_specs=in_specs, out_specs=out_specs, out_shape=out_shape,
        scratch_shapes=[pltpu.VMEM((tm + 8, 2 * mw), F32)],
        compiler_params=pltpu.CompilerParams(
            dimension_semantics=("arbitrary", "arbitrary"), vmem_limit_bytes=VMEM_LIMIT_BYTES),
        name="proj",
    )(x, gmix, *w_parts, w_gt, gate_bias, gq, gk, conv_w, conv_b)


def _scan_kernel(g_ref, o_ref, *, chunk):
    x = g_ref[...]
    S = x.shape[1]
    lane = lax.broadcasted_iota(jnp.int32, x.shape, 1)
    row = lax.broadcasted_iota(jnp.int32, x.shape, 0)
    glob = x
    sh = 1
    while sh < S:
        glob = glob + jnp.where(lane >= sh, pltpu.roll(glob, sh, axis=1), 0.0)
        sh *= 2
    seg = x
    lane_in = lane & (chunk - 1)
    sh = 1
    while sh < chunk:
        seg = seg + jnp.where(lane_in >= sh, pltpu.roll(seg, sh, axis=1), 0.0)
        sh *= 2
    o_ref[...] = jnp.where(row < 8, glob, jnp.where(row < 12, x, seg))


def _scan_call(gates, *, chunk):
    B, R, S = gates.shape
    spec = pl.BlockSpec((None, R, S), lambda b: (b, 0, 0))
    return pl.pallas_call(
        functools.partial(_scan_kernel, chunk=chunk),
        grid=(B,), in_specs=[spec], out_specs=spec,
        out_shape=jax.ShapeDtypeStruct(gates.shape, F32),
        compiler_params=pltpu.CompilerParams(dimension_semantics=("arbitrary",)),
        name="gate_scan",
    )(gates)


def _split3(x):
    hi = x.astype(BF16).astype(F32)
    r = x - hi
    mid = r.astype(BF16).astype(F32)
    lo = (r - mid).astype(BF16).astype(F32)
    return hi, mid, lo


def _fox_kernel(cs_ref, q_ref, k_ref, v_ref, c_ref, o_ref,
                kaug_ref, vtaug_ref, qt_ref, acc_ref, m_ref, s0_ref, s1_ref, p0_ref, p1_ref, *, t, nk, heads):
    b = pl.program_id(0)
    i = pl.program_id(1)
    cs_base = b * heads * nk
    half = FOX_HEAD_DIM
    pairs = heads // 2

    def cs(h, j):
        return cs_ref[cs_base + h * nk + j]

    bound = cs_ref[pl.num_programs(0) * heads * nk]
    safe = 2.0 * bound < SAFE_EXP2_RANGE

    row = lax.broadcasted_iota(jnp.int32, (LANES, t), 0)
    top = row < half
    lane = lax.broadcasted_iota(jnp.int32, (1, LANES), 1)
    lo_mask = lane < half
    row8 = lax.broadcasted_iota(jnp.int32, (8, t), 0)
    ones_a = jnp.where(row < half + 3, 1.0, 0.0)
    ones_b = jnp.where(row < 3, 1.0, 0.0)
    pad = jnp.zeros((half - 8, t), F32)

    acc_ref[...] = jnp.zeros(acc_ref.shape, F32)

    def causal_mask(s):
        kr = lax.broadcasted_iota(jnp.int32, (t, t), 0)
        qc = lax.broadcasted_iota(jnp.int32, (t, t), 1)
        return jnp.where(kr <= qc, s, NEG_BIG)

    def shift_row(h, j):
        return (cs(h, j) - c_ref[h, pl.ds(i, 1), :]) * LOG2E + bound

    def prep_pair(pp):
        lanes = slice(pp * LANES, (pp + 1) * LANES)
        q_t = q_ref[:, lanes].astype(F32).T
        qs = (jnp.where(top, q_t, ones_a).astype(BF16), jnp.where(top, ones_b, q_t).astype(BF16))
        pieces = []
        for e in range(2):
            h = 2 * pp + e
            qt_ref[h] = qs[e]
            brel = (cs(h, i) - c_ref[h, pl.ds(i, 1), :]) * LOG2E
            hi, mid, lo = _split3(brel)
            pieces.append(jnp.where(row8 == 0, hi, jnp.where(row8 == 1, mid, jnp.where(row8 == 2, lo, 0.0))))
        extra_t = jnp.concatenate([pieces[1], pad, pieces[0], pad], axis=0).T.astype(BF16)
        k2 = k_ref[:, lanes]
        ks = (jnp.where(lo_mask, k2, extra_t), jnp.where(lo_mask, extra_t, k2))
        v_t = v_ref[:, lanes].astype(F32).T.astype(BF16)
        one = jnp.ones_like(v_t)
        vs = (jnp.where(top, v_t, one), jnp.where(top, one, v_t))
        for e in range(2):
            kaug_ref[2 * pp + e, i] = ks[e]
            vtaug_ref[2 * pp + e, i] = vs[e]
        return qs, ks

    @pl.when(safe)
    def _():
        for pp in range(pairs):
            qs, ks = prep_pair(pp)
            for e in range(2):
                h = 2 * pp + e
                s = causal_mask(jnp.dot(ks[e], qs[e], preferred_element_type=F32))
                p0_ref[h] = jnp.exp2(s - shift_row(h, i)).astype(BF16)

        def pexp(j, p_ref):
            for h in range(heads):
                s = jnp.dot(kaug_ref[h, j], qt_ref[h], preferred_element_type=F32)
                p_ref[h] = jnp.exp2(s - shift_row(h, j)).astype(BF16)

        def pv(j, p_ref):
            for h in range(heads):
                acc_ref[h] += jnp.dot(vtaug_ref[h, j], p_ref[h], preferred_element_type=F32)

        def pair_body(n, carry):
            j = 2 * n
            pexp(j, p1_ref)
            pv(jnp.where(n == 0, i, j - 1), p0_ref)
            pexp(j + 1, p0_ref)
            pv(j, p1_ref)
            return carry

        lax.fori_loop(0, i // 2, pair_body, 0)

        @pl.when(i % 2 == 0)
        def _():
            pv(jnp.where(i == 0, i, i - 1), p0_ref)

        @pl.when(i % 2 == 1)
        def _():
            pexp(i - 1, p1_ref)
            pv(jnp.where(i == 1, i, i - 2), p0_ref)
            pv(i - 1, p1_ref)

    @pl.when(jnp.logical_not(safe))
    def _():
        for pp in range(pairs):
            prep_pair(pp)

        def one_pair(pp, carry):
            def scores(j, s_ref):
                for e in range(2):
                    s_ref[e] = jnp.dot(kaug_ref[2 * pp + e, j], qt_ref[2 * pp + e], preferred_element_type=F32)

            def consume(j, s_ref, masked):
                for e in range(2):
                    h = 2 * pp + e
                    s = s_ref[e]
                    if masked:
                        s = causal_mask(s)
                    d = (cs(h, j) - cs(h, i)) * LOG2E
                    m_old = m_ref[e]
                    m_new = jnp.maximum(m_old, jnp.max(s, axis=0, keepdims=True) - d)
                    alpha = jnp.exp2(m_old - m_new)
                    p = jnp.exp2(s - (m_new + d))
                    acc_ref[h] = alpha * acc_ref[h] + jnp.dot(vtaug_ref[h, j], p.astype(BF16),
                                                             preferred_element_type=F32)
                    m_ref[e] = m_new

            m_ref[...] = jnp.full(m_ref.shape, NEG_BIG, F32)
            scores(0, s0_ref)

            def pair_body(n, c2):
                j = 2 * n
                scores(j + 1, s1_ref)
                consume(j, s0_ref, False)
                scores(j + 2, s0_ref)
                consume(j + 1, s1_ref, False)
                return c2

            lax.fori_loop(0, i // 2, pair_body, 0)

            @pl.when(i % 2 == 0)
            def _():
                consume(i, s0_ref, True)

            @pl.when(i % 2 == 1)
            def _():
                scores(i, s1_ref)
                consume(i - 1, s0_ref, False)
                consume(i, s1_ref, True)

            return carry

        lax.fori_loop(0, pairs, one_pair, 0)

    for pp in range(pairs):
        a0 = acc_ref[2 * pp]
        a1 = acc_ref[2 * pp + 1]
        out_t = jnp.concatenate([a0[:half] / a0[half:], a1[half:] / a1[:half]], axis=0)
        o_ref[:, pp * LANES:(pp + 1) * LANES] = out_t.T.astype(o_ref.dtype)


def _fox_call(qa, ka, va, c4, cstart, *, t):
    B, S, W = qa.shape
    heads = W // FOX_HEAD_DIM
    nk = S // t
    tile = pl.BlockSpec((None, t, W), lambda b, i: (b, i, 0))
    cspec = pl.BlockSpec((None, heads, nk, t), lambda b, i: (b, 0, 0, 0))
    return pl.pallas_call(
        functools.partial(_fox_kernel, t=t, nk=nk, heads=heads),
        grid=(B, nk),
        in_specs=[pl.BlockSpec(memory_space=pltpu.SMEM), tile, tile, tile, cspec],
        out_specs=tile,
        out_shape=jax.ShapeDtypeStruct((B, S, W), BF16),
        scratch_shapes=[pltpu.VMEM((heads, nk, t, LANES), BF16), pltpu.VMEM((heads, nk, LANES, t), BF16),
                        pltpu.VMEM((heads, LANES, t), BF16), pltpu.VMEM((heads, LANES, t), F32),
                        pltpu.VMEM((2, 1, t), F32),
                        pltpu.VMEM((2, t, t), F32), pltpu.VMEM((2, t, t), F32),
                        pltpu.VMEM((heads, t, t), BF16), pltpu.VMEM((heads, t, t), BF16)],
        compiler_params=pltpu.CompilerParams(
            dimension_semantics=("arbitrary", "arbitrary"), vmem_limit_bytes=VMEM_LIMIT_BYTES),
        name="fox_attn",
    )(cstart, qa, ka, va, c4)


def _mlstm_stages(q_ref, kt_ref, v_ref, og_ref, b_ref, li_ref, gh_ref, c_ref, m_ref, out_ref, tile, *,
                  L, heads, nchunk):
    d = LANES
    r = lax.broadcasted_iota(jnp.int32, (L, L), 0)
    c = lax.broadcasted_iota(jnp.int32, (L, L), 1)
    causal = c <= r
    eye = c == r
    ones = jnp.ones((L, d), BF16)
    items = [(cc, hh) for cc in range(nchunk) for hh in range(heads)]
    col = lambda hh: slice(hh * d, (hh + 1) * d)
    rows = lambda cc: slice(cc * L, (cc + 1) * L)
    q_of = lambda it: q_ref[rows(it[0]), col(it[1])]
    kt_of = lambda it: kt_ref[col(it[1]), rows(it[0])]
    v_aug_of = lambda it: jnp.concatenate([v_ref[rows(it[0]), col(it[1])], ones], axis=1)
    st = {}

    def stage_scores():
        st["s_raw"] = {it: jnp.dot(q_of(it), kt_of(it), preferred_element_type=F32) for it in items}

    def stage_updates():
        b_row, u_row = {}, {}
        for cc, hh in items:
            ci = tile * nchunk + cc
            b_row[cc, hh] = b_ref[hh, pl.ds(ci, 1), :]
            u_row[cc, hh] = li_ref[hh, pl.ds(ci, 1), :] - b_row[cc, hh]
        st["b_row"], st["u_row"] = b_row, u_row
        st["b_last"] = {it: b_row[it][:, L - 1:L] for it in items}
        u_max = {it: jnp.max(u_row[it], axis=1, keepdims=True) for it in items}
        st["m_loc"] = {it: st["b_last"][it] + u_max[it] for it in items}
        kw = {it: (kt_of(it).astype(F32) * _exp(u_row[it] - u_max[it])).astype(BF16) for it in items}
        st["upd"] = {it: jnp.dot(kw[it], v_aug_of(it), preferred_element_type=F32) for it in items}

    def stage_intra():
        b_col = {it: jnp.sum(jnp.where(eye, st["b_row"][it], 0.0), axis=1, keepdims=True) for it in items}
        dmat = {it: jnp.where(causal, b_col[it] + st["u_row"][it], NEG_BIG) for it in items}
        st["b_col"] = b_col
        st["m_d"] = {it: jnp.max(dmat[it], axis=1, keepdims=True) for it in items}
        s = {it: (st["s_raw"][it] * _exp(dmat[it] - st["m_d"][it])).astype(BF16) for it in items}
        st["intra"] = {it: jnp.dot(s[it], v_aug_of(it), preferred_element_type=F32) for it in items}

    def stage_inter():
        c_state = [c_ref[hh] for hh in range(heads)]
        m_prev = [m_ref[hh] for hh in range(heads)]
        st["m_in"], c_in = {}, {}
        for cc, hh in items:
            st["m_in"][cc, hh] = m_prev[hh]
            c_in[cc, hh] = c_state[hh]
            m_new = jnp.maximum(st["b_last"][cc, hh] + m_prev[hh], st["m_loc"][cc, hh])
            decay = _exp(st["b_last"][cc, hh] + m_prev[hh] - m_new)
            gain = _exp(st["m_loc"][cc, hh] - m_new)
            c_state[hh] = jnp.tile(decay, (1, 2)) * c_state[hh] + jnp.tile(gain, (1, 2)) * st["upd"][cc, hh]
            m_prev[hh] = m_new
        for hh in range(heads):
            c_ref[hh] = c_state[hh]
            m_ref[hh] = m_prev[hh]
        st["inter"] = {it: jnp.dot(q_of(it), c_in[it].astype(BF16), preferred_element_type=F32)
                       for it in items}

    def stage_output():
        for cc, hh in items:
            a = st["b_col"][cc, hh] + st["m_in"][cc, hh]
            m_t = jnp.maximum(a, st["m_d"][cc, hh])
            w_inter = _exp(a - m_t)
            w_intra = _exp(st["m_d"][cc, hh] - m_t)
            inter, intra = st["inter"][cc, hh], st["intra"][cc, hh]
            num = w_inter * inter[:, :d] + w_intra * intra[:, :d]
            den = w_inter * inter[:, d:] + w_intra * intra[:, d:]
            hval = num / jnp.maximum(jnp.abs(den), _exp(-m_t))
            mu = jnp.mean(hval, axis=-1, keepdims=True)
            xc = hval - mu
            yn = xc * lax.rsqrt(jnp.mean(xc * xc, axis=-1, keepdims=True) + EPS)
            out_ref[rows(cc), col(hh)] = (yn * gh_ref[:, col(hh)]
                                          * og_ref[rows(cc), col(hh)].astype(F32)).astype(out_ref.dtype)

    return [stage_scores, stage_updates, stage_intra, stage_inter, stage_output]


def _tail_kernel(x_ref, ya_ref, mga_ref, mgb_ref, q0_ref, kt0_ref, v0_ref, og0_ref,
                 qn_ref, ktn_ref, vn_ref, ogn_ref, b_ref, li_ref, gh_ref,
                 wfo_ref, wmo_ref, wo_ref, gffn_ref, wg_ref, wu_ref, wd_ref, o_ref,
                 c_ref, m_ref, yb_ref, *, L, heads, nchunk):
    i = pl.program_id(1)
    last = pl.num_programs(1) - 1
    cur = i % 2
    kw = dict(L=L, heads=heads, nchunk=nchunk)

    @pl.when(i == 0)
    def _():
        c_ref[...] = jnp.zeros(c_ref.shape, F32)
        m_ref[...] = jnp.zeros(m_ref.shape, F32)
        for stage in _mlstm_stages(q0_ref, kt0_ref, v0_ref, og0_ref, b_ref, li_ref, gh_ref, c_ref, m_ref,
                                   yb_ref.at[0], 0, **kw):
            stage()

    stages = _mlstm_stages(qn_ref, ktn_ref, vn_ref, ogn_ref, b_ref, li_ref, gh_ref, c_ref, m_ref,
                           yb_ref.at[1 - cur], jnp.minimum(i + 1, last), **kw)

    pa = jnp.dot(ya_ref[...], wfo_ref[...], preferred_element_type=F32)
    stages[0]()
    pb = jnp.dot(yb_ref[cur], wmo_ref[...], preferred_element_type=F32)
    stages[1]()
    merged = mga_ref[...].astype(F32) * pa + mgb_ref[...].astype(F32) * pb
    x1 = x_ref[...] + jnp.dot(merged.astype(BF16), wo_ref[...], preferred_element_type=F32)
    stages[2]()
    h2 = (x1 * lax.rsqrt(jnp.mean(x1 * x1, axis=-1, keepdims=True) + EPS) * gffn_ref[...]).astype(BF16)
    d_ff = wg_ref.shape[1]
    split = (d_ff // 2 + MXU_TILE - 1) // MXU_TILE * MXU_TILE
    acts = []
    for cols, after_gate in ((slice(0, split), stages[3]), (slice(split, d_ff), stages[4])):
        gate = jnp.dot(h2, wg_ref[:, cols], preferred_element_type=F32)
        after_gate()
        up = jnp.dot(h2, wu_ref[:, cols], preferred_element_type=F32)
        acts.append((gate * _sigmoid(gate) * up).astype(BF16))
    act = jnp.concatenate(acts, axis=1)
    o_ref[...] = x1 + jnp.dot(act, wd_ref[...], preferred_element_type=F32)


def _tail_call(x, ya, mga, mgb, qb, kbt, vb, og, bcs, li, gh, wfo, wmo, wo, gffn, wg, wu, wd, *, tm, L):
    B, S, D = x.shape
    W = qb.shape[2]
    heads = W // LANES
    nt = S // tm
    tok = lambda width: pl.BlockSpec((None, tm, width), lambda b, i: (b, i, 0))
    first = pl.BlockSpec((None, tm, W), lambda b, i: (b, 0, 0), pipeline_mode=pl.Buffered(1))
    first_t = pl.BlockSpec((None, W, tm), lambda b, i: (b, 0, 0), pipeline_mode=pl.Buffered(1))
    nxt = pl.BlockSpec((None, tm, W), lambda b, i: (b, jnp.minimum(i + 1, nt - 1), 0))
    nxt_t = pl.BlockSpec((None, W, tm), lambda b, i: (b, 0, jnp.minimum(i + 1, nt - 1)))
    gspec = pl.BlockSpec((None, heads, S // L, L), lambda b, i: (b, 0, 0, 0))
    in_specs = [tok(D), tok(ya.shape[2]), tok(D), tok(D),
                first, first_t, first, first, nxt, nxt_t, nxt, nxt, gspec, gspec, _const_spec(gh.shape),
                _const_spec(wfo.shape), _const_spec(wmo.shape), _const_spec(wo.shape), _const_spec(gffn.shape),
                _const_spec(wg.shape), _const_spec(wu.shape), _const_spec(wd.shape)]
    return pl.pallas_call(
        functools.partial(_tail_kernel, L=L, heads=heads, nchunk=tm // L),
        grid=(B, nt), in_specs=in_specs, out_specs=tok(D),
        out_shape=jax.ShapeDtypeStruct((B, S, D), x.dtype),
        scratch_shapes=[pltpu.VMEM((heads, LANES, 2 * LANES), F32), pltpu.VMEM((heads, 1, LANES), F32),
                        pltpu.VMEM((2, tm, W), BF16)],
        compiler_params=pltpu.CompilerParams(
            dimension_semantics=("arbitrary", "arbitrary"), vmem_limit_bytes=TAIL_VMEM_LIMIT_BYTES),
        name="tail",
    )(x, ya, mga, mgb, qb, kbt, vb, og, qb, kbt, vb, og, bcs, li, gh, wfo, wmo, wo, gffn, wg, wu, wd)


def _block(x, g_mix, w_in, b_fox_f, g_q_fox, g_k_fox, conv_w, conv_b, b_mlstm_i, b_mlstm_f,
           g_mlstm_h, w_fox_out, w_mlstm_out, w_o, g_ffn, w_gate, w_up, w_down):
    B, S, D = x.shape
    fw = w_fox_out.shape[0]
    mw = w_mlstm_out.shape[0]
    fh = fw // FOX_HEAD_DIM
    assert fh == 8 and mw // LANES == MLSTM_HEADS and fw % LANES == 0
    tm = min(512, S)
    t_attn = min(512, S)
    chunk = min(128, S)
    assert S % tm == 0 and S % t_attn == 0 and tm % chunk == 0

    sizes = (fw, fw, fw, fh, mw, mw, mw, MLSTM_HEADS, MLSTM_HEADS, mw, D, D)
    offs = [0]
    for sz in sizes:
        offs.append(offs[-1] + sz)
    col = lambda k: w_in[:, offs[k]:offs[k + 1]]
    w_parts = _wsplit_call(w_in, [(offs[k], offs[k + 1]) for k in (0, 1, 2, 4, 5, 6, 9, 10, 11)],
                           rows=min(128, D))
    w_gt = jnp.concatenate([col(3), col(7), col(8)], axis=1).T.astype(BF16)
    gate_bias = jnp.concatenate([b_fox_f, b_mlstm_i, b_mlstm_f]).astype(F32)[:, None]
    root_d = FOX_HEAD_DIM ** 0.5
    gq = jnp.tile(g_q_fox.astype(F32) * (root_d * FOX_HEAD_DIM ** -0.5 * LOG2E), fh)[None, :]
    gk = jnp.tile(g_k_fox.astype(F32) * root_d, fh)[None, :]

    (qa, ka, va, qb, kbt, vb, og, mga, mgb, gates) = _proj_call(
        x, g_mix.astype(F32)[None, :], w_parts, w_gt, gate_bias, gq, gk,
        conv_w.astype(F32), conv_b.astype(F32)[None, :], tm=tm, nsub=2 if tm % 512 == 0 else 1)

    scans = _scan_call(gates, chunk=chunk)
    c4 = scans[:, :fh].reshape(B, fh, S // t_attn, t_attn)
    li = scans[:, fh:fh + MLSTM_HEADS].reshape(B, MLSTM_HEADS, S // chunk, chunk)
    bcs = scans[:, fh + MLSTM_HEADS:].reshape(B, MLSTM_HEADS, S // chunk, chunk)

    qk_bound = (FOX_HEAD_DIM * (FOX_HEAD_DIM ** -0.5 * LOG2E) * 1.02
                * jnp.max(jnp.abs(g_q_fox.astype(F32))) * jnp.max(jnp.abs(g_k_fox.astype(F32))))
    cstart = jnp.concatenate([c4[..., 0].reshape(-1), qk_bound[None]])
    ya = _fox_call(qa, ka, va, c4, cstart, t=t_attn)
    return _tail_call(x, ya, mga, mgb, qb, kbt, vb, og, bcs, li, g_mlstm_h.astype(F32)[None, :],
                      w_fox_out.astype(BF16), w_mlstm_out.astype(BF16), w_o.astype(BF16),
                      g_ffn.astype(F32)[None, :], w_gate.astype(BF16), w_up.astype(BF16),
                      w_down.astype(BF16), tm=tm, L=chunk)


def kernel(x, g_mix, w_in, b_fox_f, g_q_fox, g_k_fox, conv_w, conv_b, b_mlstm_i, b_mlstm_f, g_mlstm_h,
           w_fox_out, w_mlstm_out, w_o, g_ffn, w_gate, w_up, w_down):
    for l in range(g_mix.shape[0]):
        x = _block(x, g_mix[l], w_in[l], b_fox_f[l], g_q_fox[l], g_k_fox[l], conv_w[l], conv_b[l],
                   b_mlstm_i[l], b_mlstm_f[l], g_mlstm_h[l], w_fox_out[l], w_mlstm_out[l], w_o[l],
                   g_ffn[l], w_gate[l], w_up[l], w_down[l])
    return x
```

```python
import functools

import jax
import jax.numpy as jnp
from jax import lax
from jax.experimental import pallas as pl
from jax.experimental.pallas import tpu as pltpu

EPS = 1e-6
FOX_HEAD_DIM = 64
MLSTM_HEADS = 4
CONV_WIDTH = 4
LANES = 128
MXU_TILE = 256
NEG_BIG = -1e30
LOG2E = 1.4426950408889634
SAFE_EXP2_RANGE = 100.0
VMEM_LIMIT_BYTES = 56 * 1024 * 1024
TAIL_VMEM_LIMIT_BYTES = 60 * 1024 * 1024

F32 = jnp.float32
BF16 = jnp.bfloat16


def _const_spec(shape):
    nd = len(shape)
    return pl.BlockSpec(shape, lambda *_: (0,) * nd, pipeline_mode=pl.Buffered(1))


def _exp(x):
    return jnp.exp2(x * LOG2E)


def _log_sigmoid(z):
    return jnp.minimum(z, 0.0) - jnp.log(1.0 + _exp(-jnp.abs(z)))


def _sigmoid(z):
    return 1.0 / (1.0 + jnp.exp2(z * (-LOG2E)))


def _head_rms(x, g_row, lo_mask):
    outs = []
    for j in range(x.shape[1] // LANES):
        v = x[:, j * LANES:(j + 1) * LANES]
        v2 = v * v
        s_lo = jnp.sum(jnp.where(lo_mask, v2, 0.0), axis=-1, keepdims=True)
        s_hi = jnp.sum(jnp.where(lo_mask, 0.0, v2), axis=-1, keepdims=True)
        outs.append(v * lax.rsqrt(jnp.where(lo_mask, s_lo, s_hi) + FOX_HEAD_DIM * EPS))
    return jnp.concatenate(outs, axis=-1) * g_row


def _proj_kernel(x_ref, gmix_ref, w_ref, wg_ref, gb_ref, gq_ref, gk_ref, cw_ref, cb_ref,
                 qa_ref, ka_ref, va_ref, qb_ref, kbt_ref, vb_ref, og_ref, mga_ref, mgb_ref, gates_ref,
                 cbuf_ref, *, tm, nsub, fw, mw, d_model, k_scale):
    i = pl.program_id(1)
    lane = lax.broadcasted_iota(jnp.int32, (1, LANES), 1)
    lo_mask = lane < FOX_HEAD_DIM

    @pl.when(i == 0)
    def _():
        cbuf_ref[0:8, :] = jnp.zeros((8, 2 * mw), F32)

    ts = tm // nsub
    for r in range(nsub):
        rows = slice(r * ts, (r + 1) * ts)
        c0 = 8 + r * ts
        x = x_ref[rows, :]
        h = x * lax.rsqrt(jnp.mean(x * x, axis=-1, keepdims=True) + EPS) * gmix_ref[...]
        hb = h.astype(BF16)

        def proj(lo, width):
            return jnp.dot(hb, w_ref[:, lo:lo + width], preferred_element_type=F32)

        o_qb = 3 * fw
        cbuf_ref[c0:c0 + ts, 0:mw] = proj(o_qb, mw)
        cbuf_ref[c0:c0 + ts, mw:2 * mw] = proj(o_qb + mw, mw)
        qa_ref[rows, :] = _head_rms(proj(0, fw), gq_ref[...], lo_mask).astype(BF16)
        ka_ref[rows, :] = _head_rms(proj(fw, fw), gk_ref[...], lo_mask).astype(BF16)
        va_ref[rows, :] = proj(2 * fw, fw).astype(BF16)
        vb_ref[rows, :] = proj(o_qb + 2 * mw, mw).astype(BF16)
        og_ref[rows, :] = _sigmoid(proj(o_qb + 3 * mw, mw)).astype(BF16)
        mga_ref[rows, :] = _sigmoid(proj(o_qb + 4 * mw, d_model)).astype(BF16)
        mgb_ref[rows, :] = _sigmoid(proj(o_qb + 4 * mw + d_model, d_model)).astype(BF16)

        z = lax.dot_general(wg_ref[...], hb, (((1,), (1,)), ((), ())), preferred_element_type=F32) + gb_ref[...]
        row = lax.broadcasted_iota(jnp.int32, z.shape, 0)
        is_input_gate = (row >= 8) & (row < 12)
        gates_ref[:, rows] = jnp.where(is_input_gate, z, _log_sigmoid(z))

        for half in range(2):
            cols = slice(half * mw, (half + 1) * mw)
            y = cb_ref[:, cols] + cw_ref[CONV_WIDTH - 1:CONV_WIDTH, cols] * cbuf_ref[c0:c0 + ts, cols]
            for j in range(CONV_WIDTH - 1):
                r0 = c0 - (CONV_WIDTH - 1) + j
                y = y + cw_ref[j:j + 1, cols] * cbuf_ref[r0:r0 + ts, cols]
            y = y * _sigmoid(y)
            if half == 0:
                qb_ref[rows, :] = y.astype(BF16)
            else:
                kbt_ref[:, rows] = (y * k_scale).T.astype(BF16)
    cbuf_ref[0:8, :] = cbuf_ref[tm:tm + 8, :]


def _proj_call(x, gmix, w_main, w_gt, gate_bias, gq, gk, conv_w, conv_b, *, tm, nsub):
    B, S, D = x.shape
    fw = gq.shape[1]
    mw = conv_w.shape[1] // 2
    n_main = w_main.shape[1]
    grid = (B, S // tm)
    tok = lambda width: pl.BlockSpec((None, tm, width), lambda b, i: (b, i, 0))
    out_shape = (
        jax.ShapeDtypeStruct((B, S, fw), BF16),
        jax.ShapeDtypeStruct((B, S, fw), BF16),
        jax.ShapeDtypeStruct((B, S, fw), BF16),
        jax.ShapeDtypeStruct((B, S, mw), BF16),
        jax.ShapeDtypeStruct((B, mw, S), BF16),
        jax.ShapeDtypeStruct((B, S, mw), BF16),
        jax.ShapeDtypeStruct((B, S, mw), BF16),
        jax.ShapeDtypeStruct((B, S, D), BF16),
        jax.ShapeDtypeStruct((B, S, D), BF16),
        jax.ShapeDtypeStruct((B, 16, S), F32),
    )
    out_specs = (
        tok(fw), tok(fw), tok(fw), tok(mw),
        pl.BlockSpec((None, mw, tm), lambda b, i: (b, 0, i)),
        tok(mw), tok(mw), tok(D), tok(D),
        pl.BlockSpec((None, 16, tm), lambda b, i: (b, 0, i)),
    )
    in_specs = [
        tok(D),
        _const_spec((1, D)),
        _const_spec((D, n_main)),
        _const_spec((16, D)),
        _const_spec((16, 1)),
        _const_spec((1, fw)),
        _const_spec((1, fw)),
        _const_spec((CONV_WIDTH, 2 * mw)),
        _const_spec((1, 2 * mw)),
    ]
    kern = functools.partial(_proj_kernel, tm=tm, nsub=nsub, fw=fw, mw=mw, d_model=D,
                             k_scale=(mw // MLSTM_HEADS) ** -0.5)
    return pl.pallas_call(
        kern, grid=grid, in_specs=in_specs, out_specs=out_specs, out_shape=out_shape,
        scratch_shapes=[pltpu.VMEM((tm + 8, 2 * mw), F32)],
        compiler_params=pltpu.CompilerParams(
            dimension_semantics=("arbitrary", "arbitrary"), vmem_limit_bytes=VMEM_LIMIT_BYTES),
        name="proj",
    )(x, gmix, w_main, w_gt, gate_bias, gq, gk, conv_w, conv_b)


def _scan_kernel(g_ref, o_ref, *, chunk):
    x = g_ref[...]
    S = x.shape[1]
    lane = lax.broadcasted_iota(jnp.int32, x.shape, 1)
    row = lax.broadcasted_iota(jnp.int32, x.shape, 0)
    glob = x
    sh = 1
    while sh < S:
        glob = glob + jnp.where(lane >= sh, pltpu.roll(glob, sh, axis=1), 0.0)
        sh *= 2
    seg = x
    lane_in = lane & (chunk - 1)
    sh = 1
    while sh < chunk:
        seg = seg + jnp.where(lane_in >= sh, pltpu.roll(seg, sh, axis=1), 0.0)
        sh *= 2
    o_ref[...] = jnp.where(row < 8, glob, jnp.where(row < 12, x, seg))


def _scan_call(gates, *, chunk):
    B, R, S = gates.shape
    spec = pl.BlockSpec((None, R, S), lambda b: (b, 0, 0))
    return pl.pallas_call(
        functools.partial(_scan_kernel, chunk=chunk),
        grid=(B,), in_specs=[spec], out_specs=spec,
        out_shape=jax.ShapeDtypeStruct(gates.shape, F32),
        compiler_params=pltpu.CompilerParams(dimension_semantics=("arbitrary",)),
        name="gate_scan",
    )(gates)


def _split3(x):
    hi = x.astype(BF16).astype(F32)
    r = x - hi
    mid = r.astype(BF16).astype(F32)
    lo = (r - mid).astype(BF16).astype(F32)
    return hi, mid, lo


def _fox_kernel(cs_ref, q_ref, k_ref, v_ref, c_ref, o_ref,
                kaug_ref, vtaug_ref, qt_ref, acc_ref, m_ref, s0_ref, s1_ref, p0_ref, p1_ref, *, t, nk, heads):
    b = pl.program_id(0)
    i = pl.program_id(1)
    cs_base = b * heads * nk
    half = FOX_HEAD_DIM
    pairs = heads // 2

    def cs(h, j):
        return cs_ref[cs_base + h * nk + j]

    bound = cs_ref[pl.num_programs(0) * heads * nk]
    safe = 2.0 * bound < SAFE_EXP2_RANGE

    row = lax.broadcasted_iota(jnp.int32, (LANES, t), 0)
    top = row < half
    lane = lax.broadcasted_iota(jnp.int32, (1, LANES), 1)
    lo_mask = lane < half
    row8 = lax.broadcasted_iota(jnp.int32, (8, t), 0)
    ones_a = jnp.where(row < half + 3, 1.0, 0.0)
    ones_b = jnp.where(row < 3, 1.0, 0.0)
    pad = jnp.zeros((half - 8, t), F32)

    acc_ref[...] = jnp.zeros(acc_ref.shape, F32)

    def causal_mask(s):
        kr = lax.broadcasted_iota(jnp.int32, (t, t), 0)
        qc = lax.broadcasted_iota(jnp.int32, (t, t), 1)
        return jnp.where(kr <= qc, s, NEG_BIG)

    def shift_row(h, j):
        return (cs(h, j) - c_ref[h, pl.ds(i, 1), :]) * LOG2E + bound

    def prep_pair(pp):
        lanes = slice(pp * LANES, (pp + 1) * LANES)
        q_t = q_ref[:, lanes].astype(F32).T
        qs = (jnp.where(top, q_t, ones_a).astype(BF16), jnp.where(top, ones_b, q_t).astype(BF16))
        pieces = []
        for e in range(2):
            h = 2 * pp + e
            qt_ref[h] = qs[e]
            brel = (cs(h, i) - c_ref[h, pl.ds(i, 1), :]) * LOG2E
            hi, mid, lo = _split3(brel)
            pieces.append(jnp.where(row8 == 0, hi, jnp.where(row8 == 1, mid, jnp.where(row8 == 2, lo, 0.0))))
        extra_t = jnp.concatenate([pieces[1], pad, pieces[0], pad], axis=0).T.astype(BF16)
        k2 = k_ref[:, lanes]
        ks = (jnp.where(lo_mask, k2, extra_t), jnp.where(lo_mask, extra_t, k2))
        v_t = v_ref[:, lanes].astype(F32).T.astype(BF16)
        one = jnp.ones_like(v_t)
        vs = (jnp.where(top, v_t, one), jnp.where(top, one, v_t))
        for e in range(2):
            kaug_ref[2 * pp + e, i] = ks[e]
            vtaug_ref[2 * pp + e, i] = vs[e]
        return qs, ks

    @pl.when(safe)
    def _():
        for pp in range(pairs):
            qs, ks = prep_pair(pp)
            for e in range(2):
                h = 2 * pp + e
                s = causal_mask(jnp.dot(ks[e], qs[e], preferred_element_type=F32))
                p0_ref[h] = jnp.exp2(s - shift_row(h, i)).astype(BF16)

        def pexp(j, p_ref):
            for h in range(heads):
                s = jnp.dot(kaug_ref[h, j], qt_ref[h], preferred_element_type=F32)
                p_ref[h] = jnp.exp2(s - shift_row(h, j)).astype(BF16)

        def pv(j, p_ref):
            for h in range(heads):
                acc_ref[h] += jnp.dot(vtaug_ref[h, j], p_ref[h], preferred_element_type=F32)

        def pair_body(n, carry):
            j = 2 * n
            pexp(j, p1_ref)
            pv(jnp.where(n == 0, i, j - 1), p0_ref)
            pexp(j + 1, p0_ref)
            pv(j, p1_ref)
            return carry

        lax.fori_loop(0, i // 2, pair_body, 0)

        @pl.when(i % 2 == 0)
        def _():
            pv(jnp.where(i == 0, i, i - 1), p0_ref)

        @pl.when(i % 2 == 1)
        def _():
            pexp(i - 1, p1_ref)
            pv(jnp.where(i == 1, i, i - 2), p0_ref)
            pv(i - 1, p1_ref)

    @pl.when(jnp.logical_not(safe))
    def _():
        for pp in range(pairs):
            prep_pair(pp)

        def one_pair(pp, carry):
            def scores(j, s_ref):
                for e in range(2):
                    s_ref[e] = jnp.dot(kaug_ref[2 * pp + e, j], qt_ref[2 * pp + e], preferred_element_type=F32)

            def consume(j, s_ref, masked):
                for e in range(2):
                    h = 2 * pp + e
                    s = s_ref[e]
                    if masked:
                        s = causal_mask(s)
                    d = (cs(h, j) - cs(h, i)) * LOG2E
                    m_old = m_ref[e]
                    m_new = jnp.maximum(m_old, jnp.max(s, axis=0, keepdims=True) - d)
                    alpha = jnp.exp2(m_old - m_new)
                    p = jnp.exp2(s - (m_new + d))
                    acc_ref[h] = alpha * acc_ref[h] + jnp.dot(vtaug_ref[h, j], p.astype(BF16),
                                                             preferred_element_type=F32)
                    m_ref[e] = m_new

            m_ref[...] = jnp.full(m_ref.shape, NEG_BIG, F32)
            scores(0, s0_ref)

            def pair_body(n, c2):
                j = 2 * n
                scores(j + 1, s1_ref)
                consume(j, s0_ref, False)
                scores(j + 2, s0_ref)
                consume(j + 1, s1_ref, False)
                return c2

            lax.fori_loop(0, i // 2, pair_body, 0)

            @pl.when(i % 2 == 0)
            def _():
                consume(i, s0_ref, True)

            @pl.when(i % 2 == 1)
            def _():
                scores(i, s1_ref)
                consume(i - 1, s0_ref, False)
                consume(i, s1_ref, True)

            return carry

        lax.fori_loop(0, pairs, one_pair, 0)

    for pp in range(pairs):
        a0 = acc_ref[2 * pp]
        a1 = acc_ref[2 * pp + 1]
        out_t = jnp.concatenate([a0[:half] / a0[half:], a1[half:] / a1[:half]], axis=0)
        o_ref[:, pp * LANES:(pp + 1) * LANES] = out_t.T.astype(o_ref.dtype)


def _fox_call(qa, ka, va, c4, cstart, *, t):
    B, S, W = qa.shape
    heads = W // FOX_HEAD_DIM
    nk = S // t
    tile = pl.BlockSpec((None, t, W), lambda b, i: (b, i, 0))
    cspec = pl.BlockSpec((None, heads, nk, t), lambda b, i: (b, 0, 0, 0))
    return pl.pallas_call(
        functools.partial(_fox_kernel, t=t, nk=nk, heads=heads),
        grid=(B, nk),
        in_specs=[pl.BlockSpec(memory_space=pltpu.SMEM), tile, tile, tile, cspec],
        out_specs=tile,
        out_shape=jax.ShapeDtypeStruct((B, S, W), BF16),
        scratch_shapes=[pltpu.VMEM((heads, nk, t, LANES), BF16), pltpu.VMEM((heads, nk, LANES, t), BF16),
                        pltpu.VMEM((heads, LANES, t), BF16), pltpu.VMEM((heads, LANES, t), F32),
                        pltpu.VMEM((2, 1, t), F32),
                        pltpu.VMEM((2, t, t), F32), pltpu.VMEM((2, t, t), F32),
                        pltpu.VMEM((heads, t, t), BF16), pltpu.VMEM((heads, t, t), BF16)],
        compiler_params=pltpu.CompilerParams(
            dimension_semantics=("arbitrary", "arbitrary"), vmem_limit_bytes=VMEM_LIMIT_BYTES),
        name="fox_attn",
    )(cstart, qa, ka, va, c4)


def _mlstm_stages(q_ref, kt_ref, v_ref, og_ref, b_ref, li_ref, gh_ref, c_ref, m_ref, out_ref, tile, fresh, *,
                  L, heads, nchunk):
    d = LANES
    r = lax.broadcasted_iota(jnp.int32, (L, L), 0)
    c = lax.broadcasted_iota(jnp.int32, (L, L), 1)
    causal = c <= r
    eye = c == r
    ones = jnp.ones((L, d), BF16)
    items = [(cc, hh) for cc in range(nchunk) for hh in range(heads)]
    col = lambda hh: slice(hh * d, (hh + 1) * d)
    rows = lambda cc: slice(cc * L, (cc + 1) * L)
    q_of = lambda it: q_ref[rows(it[0]), col(it[1])]
    kt_of = lambda it: kt_ref[col(it[1]), rows(it[0])]
    v_aug_of = lambda it: jnp.concatenate([v_ref[rows(it[0]), col(it[1])], ones], axis=1)
    st = {}

    def stage_scores():
        st["s_raw"] = {it: jnp.dot(q_of(it), kt_of(it), preferred_element_type=F32) for it in items}

    def stage_updates():
        b_row, u_row = {}, {}
        for cc, hh in items:
            ci = tile * nchunk + cc
            b_row[cc, hh] = b_ref[hh, pl.ds(ci, 1), :]
            u_row[cc, hh] = li_ref[hh, pl.ds(ci, 1), :] - b_row[cc, hh]
        st["b_row"], st["u_row"] = b_row, u_row
        st["b_last"] = {it: b_row[it][:, L - 1:L] for it in items}
        u_max = {it: jnp.max(u_row[it], axis=1, keepdims=True) for it in items}
        st["m_loc"] = {it: st["b_last"][it] + u_max[it] for it in items}
        kw = {it: (kt_of(it).astype(F32) * _exp(u_row[it] - u_max[it])).astype(BF16) for it in items}
        st["upd"] = {it: jnp.dot(kw[it], v_aug_of(it), preferred_element_type=F32) for it in items}

    def stage_intra():
        b_col = {it: jnp.sum(jnp.where(eye, st["b_row"][it], 0.0), axis=1, keepdims=True) for it in items}
        dmat = {it: jnp.where(causal, b_col[it] + st["u_row"][it], NEG_BIG) for it in items}
        st["b_col"] = b_col
        st["m_d"] = {it: jnp.max(dmat[it], axis=1, keepdims=True) for it in items}
        s = {it: (st["s_raw"][it] * _exp(dmat[it] - st["m_d"][it])).astype(BF16) for it in items}
        st["intra"] = {it: jnp.dot(s[it], v_aug_of(it), preferred_element_type=F32) for it in items}

    def stage_inter():
        c_state = [jnp.where(fresh, 0.0, c_ref[hh]) for hh in range(heads)]
        m_prev = [jnp.where(fresh, 0.0, m_ref[hh]) for hh in range(heads)]
        st["m_in"], c_in = {}, {}
        for cc, hh in items:
            st["m_in"][cc, hh] = m_prev[hh]
            c_in[cc, hh] = c_state[hh]
            m_new = jnp.maximum(st["b_last"][cc, hh] + m_prev[hh], st["m_loc"][cc, hh])
            decay = _exp(st["b_last"][cc, hh] + m_prev[hh] - m_new)
            gain = _exp(st["m_loc"][cc, hh] - m_new)
            c_state[hh] = jnp.tile(decay, (1, 2)) * c_state[hh] + jnp.tile(gain, (1, 2)) * st["upd"][cc, hh]
            m_prev[hh] = m_new
        for hh in range(heads):
            c_ref[hh] = c_state[hh]
            m_ref[hh] = m_prev[hh]
        st["inter"] = {it: jnp.dot(q_of(it), c_in[it].astype(BF16), preferred_element_type=F32)
                       for it in items}

    def stage_output():
        for cc, hh in items:
            a = st["b_col"][cc, hh] + st["m_in"][cc, hh]
            m_t = jnp.maximum(a, st["m_d"][cc, hh])
            w_inter = _exp(a - m_t)
            w_intra = _exp(st["m_d"][cc, hh] - m_t)
            inter, intra = st["inter"][cc, hh], st["intra"][cc, hh]
            num = w_inter * inter[:, :d] + w_intra * intra[:, :d]
            den = w_inter * inter[:, d:] + w_intra * intra[:, d:]
            hval = num / jnp.maximum(jnp.abs(den), _exp(-m_t))
            mu = jnp.mean(hval, axis=-1, keepdims=True)
            xc = hval - mu
            yn = xc * lax.rsqrt(jnp.mean(xc * xc, axis=-1, keepdims=True) + EPS)
            out_ref[rows(cc), col(hh)] = (yn * gh_ref[:, col(hh)]
                                          * og_ref[rows(cc), col(hh)].astype(F32)).astype(out_ref.dtype)

    return [stage_scores, stage_updates, stage_intra, stage_inter, stage_output]


def _tail_kernel(x_ref, ya_ref, mga_ref, mgb_ref, q0_ref, kt0_ref, v0_ref, og0_ref,
                 qn_ref, ktn_ref, vn_ref, ogn_ref, b_ref, li_ref, gh_ref,
                 wfo_ref, wmo_ref, wo_ref, gffn_ref, wg_ref, wu_ref, wd_ref, o_ref,
                 c_ref, m_ref, yb_ref, *, L, heads, nchunk):
    b = pl.program_id(0)
    i = pl.program_id(1)
    row_end = i == pl.num_programs(1) - 1
    cur = (b * pl.num_programs(1) + i) % 2
    kw = dict(L=L, heads=heads, nchunk=nchunk)

    @pl.when((b == 0) & (i == 0))
    def _():
        for stage in _mlstm_stages(q0_ref, kt0_ref, v0_ref, og0_ref, b_ref, li_ref, gh_ref, c_ref, m_ref,
                                   yb_ref.at[0], 0, True, **kw):
            stage()

    stages = _mlstm_stages(qn_ref, ktn_ref, vn_ref, ogn_ref, b_ref, li_ref, gh_ref, c_ref, m_ref,
                           yb_ref.at[1 - cur], jnp.where(row_end, 0, i + 1), row_end, **kw)

    pa = jnp.dot(ya_ref[...], wfo_ref[...], preferred_element_type=F32)
    stages[0]()
    pb = jnp.dot(yb_ref[cur], wmo_ref[...], preferred_element_type=F32)
    stages[1]()
    merged = mga_ref[...].astype(F32) * pa + mgb_ref[...].astype(F32) * pb
    x1 = x_ref[...] + jnp.dot(merged.astype(BF16), wo_ref[...], preferred_element_type=F32)
    stages[2]()
    h2 = (x1 * lax.rsqrt(jnp.mean(x1 * x1, axis=-1, keepdims=True) + EPS) * gffn_ref[...]).astype(BF16)
    d_ff = wg_ref.shape[1]
    split = (d_ff // 2 + MXU_TILE - 1) // MXU_TILE * MXU_TILE
    acts = []
    for cols, after_gate in ((slice(0, split), stages[3]), (slice(split, d_ff), stages[4])):
        gate = jnp.dot(h2, wg_ref[:, cols], preferred_element_type=F32)
        after_gate()
        up = jnp.dot(h2, wu_ref[:, cols], preferred_element_type=F32)
        acts.append((gate * _sigmoid(gate) * up).astype(BF16))
    act = jnp.concatenate(acts, axis=1)
    o_ref[...] = x1 + jnp.dot(act, wd_ref[...], preferred_element_type=F32)


def _tail_call(x, ya, mga, mgb, qb, kbt, vb, og, bcs, li, gh, wfo, wmo, wo, gffn, wg, wu, wd, *, tm, L):
    B, S, D = x.shape
    W = qb.shape[2]
    heads = W // LANES
    nt = S // tm
    tok = lambda width: pl.BlockSpec((None, tm, width), lambda b, i: (b, i, 0))
    first = pl.BlockSpec((None, tm, W), lambda b, i: (0, 0, 0), pipeline_mode=pl.Buffered(1))
    first_t = pl.BlockSpec((None, W, tm), lambda b, i: (0, 0, 0), pipeline_mode=pl.Buffered(1))
    n_row = lambda b, i: jnp.where(i == nt - 1, jnp.minimum(b + 1, B - 1), b)
    n_tile = lambda b, i: jnp.where(i == nt - 1, 0, i + 1)
    nxt = pl.BlockSpec((None, tm, W), lambda b, i: (n_row(b, i), n_tile(b, i), 0))
    nxt_t = pl.BlockSpec((None, W, tm), lambda b, i: (n_row(b, i), 0, n_tile(b, i)))
    gspec = pl.BlockSpec((None, heads, S // L, L), lambda b, i: (n_row(b, i), 0, 0, 0))
    in_specs = [tok(D), tok(ya.shape[2]), tok(D), tok(D),
                first, first_t, first, first, nxt, nxt_t, nxt, nxt, gspec, gspec, _const_spec(gh.shape),
                _const_spec(wfo.shape), _const_spec(wmo.shape), _const_spec(wo.shape), _const_spec(gffn.shape),
                _const_spec(wg.shape), _const_spec(wu.shape), _const_spec(wd.shape)]
    return pl.pallas_call(
        functools.partial(_tail_kernel, L=L, heads=heads, nchunk=tm // L),
        grid=(B, nt), in_specs=in_specs, out_specs=tok(D),
        out_shape=jax.ShapeDtypeStruct((B, S, D), x.dtype),
        scratch_shapes=[pltpu.VMEM((heads, LANES, 2 * LANES), F32), pltpu.VMEM((heads, 1, LANES), F32),
                        pltpu.VMEM((2, tm, W), BF16)],
        compiler_params=pltpu.CompilerParams(
            dimension_semantics=("arbitrary", "arbitrary"), vmem_limit_bytes=TAIL_VMEM_LIMIT_BYTES),
        name="tail",
    )(x, ya, mga, mgb, qb, kbt, vb, og, qb, kbt, vb, og, bcs, li, gh, wfo, wmo, wo, gffn, wg, wu, wd)


def _block(x, g_mix, w_in, b_fox_f, g_q_fox, g_k_fox, conv_w, conv_b, b_mlstm_i, b_mlstm_f,
           g_mlstm_h, w_fox_out, w_mlstm_out, w_o, g_ffn, w_gate, w_up, w_down):
    B, S, D = x.shape
    fw = w_fox_out.shape[0]
    mw = w_mlstm_out.shape[0]
    fh = fw // FOX_HEAD_DIM
    assert fh == 8 and mw // LANES == MLSTM_HEADS and fw % LANES == 0
    tm = min(512, S)
    t_attn = min(512, S)
    chunk = min(128, S)
    assert S % tm == 0 and S % t_attn == 0 and tm % chunk == 0

    sizes = (fw, fw, fw, fh, mw, mw, mw, MLSTM_HEADS, MLSTM_HEADS, mw, D, D)
    offs = [0]
    for sz in sizes:
        offs.append(offs[-1] + sz)
    col = lambda k: w_in[:, offs[k]:offs[k + 1]]
    w_main = jnp.concatenate([col(0), col(1), col(2), col(4), col(5), col(6), col(9), col(10), col(11)],
                             axis=1).astype(BF16)
    w_gt = jnp.concatenate([col(3), col(7), col(8)], axis=1).T.astype(BF16)
    gate_bias = jnp.concatenate([b_fox_f, b_mlstm_i, b_mlstm_f]).astype(F32)[:, None]
    root_d = FOX_HEAD_DIM ** 0.5
    gq = jnp.tile(g_q_fox.astype(F32) * (root_d * FOX_HEAD_DIM ** -0.5 * LOG2E), fh)[None, :]
    gk = jnp.tile(g_k_fox.astype(F32) * root_d, fh)[None, :]

    (qa, ka, va, qb, kbt, vb, og, mga, mgb, gates) = _proj_call(
        x, g_mix.astype(F32)[None, :], w_main, w_gt, gate_bias, gq, gk,
        conv_w.astype(F32), conv_b.astype(F32)[None, :], tm=tm, nsub=2 if tm % 512 == 0 else 1)

    scans = _scan_call(gates, chunk=chunk)
    c4 = scans[:, :fh].reshape(B, fh, S // t_attn, t_attn)
    li = scans[:, fh:fh + MLSTM_HEADS].reshape(B, MLSTM_HEADS, S // chunk, chunk)
    bcs = scans[:, fh + MLSTM_HEADS:].reshape(B, MLSTM_HEADS, S // chunk, chunk)

    qk_bound = (FOX_HEAD_DIM * (FOX_HEAD_DIM ** -0.5 * LOG2E) * 1.02
                * jnp.max(jnp.abs(g_q_fox.astype(F32))) * jnp.max(jnp.abs(g_k_fox.astype(F32))))
    cstart = jnp.concatenate([c4[..., 0].reshape(-1), qk_bound[None]])
    ya = _fox_call(qa, ka, va, c4, cstart, t=t_attn)
    return _tail_call(x, ya, mga, mgb, qb, kbt, vb, og, bcs, li, g_mlstm_h.astype(F32)[None, :],
                      w_fox_out.astype(BF16), w_mlstm_out.astype(BF16), w_o.astype(BF16),
                      g_ffn.astype(F32)[None, :], w_gate.astype(BF16), w_up.astype(BF16),
                      w_down.astype(BF16), tm=tm, L=chunk)


def kernel(x, g_mix, w_in, b_fox_f, g_q_fox, g_k_fox, conv_w, conv_b, b_mlstm_i, b_mlstm_f, g_mlstm_h,
           w_fox_out, w_mlstm_out, w_o, g_ffn, w_gate, w_up, w_down):
    for l in range(g_mix.shape[0]):
        x = _block(x, g_mix[l], w_in[l], b_fox_f[l], g_q_fox[l], g_k_fox[l], conv_w[l], conv_b[l],
                   b_mlstm_i[l], b_mlstm_f[l], g_mlstm_h[l], w_fox_out[l], w_mlstm_out[l], w_o[l],
                   g_ffn[l], w_gate[l], w_up[l], w_down[l])
    return x
```

```python
import functools

import jax
import jax.numpy as jnp
from jax import lax
from jax.experimental import pallas as pl
from jax.experimental.pallas import tpu as pltpu

EPS = 1e-6
FOX_HEAD_DIM = 64
MLSTM_HEADS = 4
CONV_WIDTH = 4
LANES = 128
MXU_TILE = 256
NEG_BIG = -1e30
LOG2E = 1.4426950408889634
SAFE_EXP2_RANGE = 100.0
VMEM_LIMIT_BYTES = 56 * 1024 * 1024
TAIL_VMEM_LIMIT_BYTES = 60 * 1024 * 1024

F32 = jnp.float32
BF16 = jnp.bfloat16


def _const_spec(shape):
    nd = len(shape)
    return pl.BlockSpec(shape, lambda *_: (0,) * nd, pipeline_mode=pl.Buffered(1))


def _exp(x):
    return jnp.exp2(x * LOG2E)


def _log_sigmoid(z):
    return jnp.minimum(z, 0.0) - jnp.log(1.0 + _exp(-jnp.abs(z)))


def _sigmoid(z):
    return 1.0 / (1.0 + jnp.exp2(z * (-LOG2E)))


def _head_rms(x, g_row, lo_mask):
    outs = []
    for j in range(x.shape[1] // LANES):
        v = x[:, j * LANES:(j + 1) * LANES]
        v2 = v * v
        s_lo = jnp.sum(jnp.where(lo_mask, v2, 0.0), axis=-1, keepdims=True)
        s_hi = jnp.sum(jnp.where(lo_mask, 0.0, v2), axis=-1, keepdims=True)
        outs.append(v * lax.rsqrt(jnp.where(lo_mask, s_lo, s_hi) + FOX_HEAD_DIM * EPS))
    return jnp.concatenate(outs, axis=-1) * g_row


def _proj_kernel(x_ref, gmix_ref, w_ref, wg_ref, gb_ref, gq_ref, gk_ref, cw_ref, cb_ref,
                 qa_ref, ka_ref, va_ref, qb_ref, kbt_ref, vb_ref, og_ref, mga_ref, mgb_ref, gates_ref,
                 cbuf_ref, *, tm, nsub, fw, mw, d_model, k_scale):
    i = pl.program_id(1)
    lane = lax.broadcasted_iota(jnp.int32, (1, LANES), 1)
    lo_mask = lane < FOX_HEAD_DIM

    @pl.when(i == 0)
    def _():
        cbuf_ref[0:8, :] = jnp.zeros((8, 2 * mw), F32)

    ts = tm // nsub
    for r in range(nsub):
        rows = slice(r * ts, (r + 1) * ts)
        c0 = 8 + r * ts
        x = x_ref[rows, :]
        h = x * lax.rsqrt(jnp.mean(x * x, axis=-1, keepdims=True) + EPS) * gmix_ref[...]
        hb = h.astype(BF16)

        def proj(lo, width):
            return jnp.dot(hb, w_ref[:, lo:lo + width], preferred_element_type=F32)

        o_qb = 3 * fw
        cbuf_ref[c0:c0 + ts, 0:mw] = proj(o_qb, mw)
        cbuf_ref[c0:c0 + ts, mw:2 * mw] = proj(o_qb + mw, mw)
        qa_ref[rows, :] = _head_rms(proj(0, fw), gq_ref[...], lo_mask).astype(BF16)
        ka_ref[rows, :] = _head_rms(proj(fw, fw), gk_ref[...], lo_mask).astype(BF16)
        va_ref[rows, :] = proj(2 * fw, fw).astype(BF16)
        vb_ref[rows, :] = proj(o_qb + 2 * mw, mw).astype(BF16)
        og_ref[rows, :] = _sigmoid(proj(o_qb + 3 * mw, mw)).astype(BF16)
        mga_ref[rows, :] = _sigmoid(proj(o_qb + 4 * mw, d_model)).astype(BF16)
        mgb_ref[rows, :] = _sigmoid(proj(o_qb + 4 * mw + d_model, d_model)).astype(BF16)

        z = lax.dot_general(wg_ref[...], hb, (((1,), (1,)), ((), ())), preferred_element_type=F32) + gb_ref[...]
        row = lax.broadcasted_iota(jnp.int32, z.shape, 0)
        is_input_gate = (row >= 8) & (row < 12)
        gates_ref[:, rows] = jnp.where(is_input_gate, z, _log_sigmoid(z))

        for half in range(2):
            cols = slice(half * mw, (half + 1) * mw)
            y = cb_ref[:, cols] + cw_ref[CONV_WIDTH - 1:CONV_WIDTH, cols] * cbuf_ref[c0:c0 + ts, cols]
            for j in range(CONV_WIDTH - 1):
                r0 = c0 - (CONV_WIDTH - 1) + j
                y = y + cw_ref[j:j + 1, cols] * cbuf_ref[r0:r0 + ts, cols]
            y = y * _sigmoid(y)
            if half == 0:
                qb_ref[rows, :] = y.astype(BF16)
            else:
                kbt_ref[:, rows] = (y * k_scale).T.astype(BF16)
    cbuf_ref[0:8, :] = cbuf_ref[tm:tm + 8, :]


def _proj_call(x, gmix, w_main, w_gt, gate_bias, gq, gk, conv_w, conv_b, *, tm, nsub):
    B, S, D = x.shape
    fw = gq.shape[1]
    mw = conv_w.shape[1] // 2
    n_main = w_main.shape[1]
    grid = (B, S // tm)
    tok = lambda width: pl.BlockSpec((None, tm, width), lambda b, i: (b, i, 0))
    out_shape = (
        jax.ShapeDtypeStruct((B, S, fw), BF16),
        jax.ShapeDtypeStruct((B, S, fw), BF16),
        jax.ShapeDtypeStruct((B, S, fw), BF16),
        jax.ShapeDtypeStruct((B, S, mw), BF16),
        jax.ShapeDtypeStruct((B, mw, S), BF16),
        jax.ShapeDtypeStruct((B, S, mw), BF16),
        jax.ShapeDtypeStruct((B, S, mw), BF16),
        jax.ShapeDtypeStruct((B, S, D), BF16),
        jax.ShapeDtypeStruct((B, S, D), BF16),
        jax.ShapeDtypeStruct((B, 16, S), F32),
    )
    out_specs = (
        tok(fw), tok(fw), tok(fw), tok(mw),
        pl.BlockSpec((None, mw, tm), lambda b, i: (b, 0, i)),
        tok(mw), tok(mw), tok(D), tok(D),
        pl.BlockSpec((None, 16, tm), lambda b, i: (b, 0, i)),
    )
    in_specs = [
        tok(D),
        _const_spec((1, D)),
        _const_spec((D, n_main)),
        _const_spec((16, D)),
        _const_spec((16, 1)),
        _const_spec((1, fw)),
        _const_spec((1, fw)),
        _const_spec((CONV_WIDTH, 2 * mw)),
        _const_spec((1, 2 * mw)),
    ]
    kern = functools.partial(_proj_kernel, tm=tm, nsub=nsub, fw=fw, mw=mw, d_model=D,
                             k_scale=(mw // MLSTM_HEADS) ** -0.5)
    return pl.pallas_call(
        kern, grid=grid, in_specs=in_specs, out_specs=out_specs, out_shape=out_shape,
        scratch_shapes=[pltpu.VMEM((tm + 8, 2 * mw), F32)],
        compiler_params=pltpu.CompilerParams(
            dimension_semantics=("arbitrary", "arbitrary"), vmem_limit_bytes=VMEM_LIMIT_BYTES),
        name="proj",
    )(x, gmix, w_main, w_gt, gate_bias, gq, gk, conv_w, conv_b)


def _scan_kernel(g_ref, o_ref, *, chunk):
    x = g_ref[...]
    S = x.shape[1]
    lane = lax.broadcasted_iota(jnp.int32, x.shape, 1)
    row = lax.broadcasted_iota(jnp.int32, x.shape, 0)
    glob = x
    sh = 1
    while sh < S:
        glob = glob + jnp.where(lane >= sh, pltpu.roll(glob, sh, axis=1), 0.0)
        sh *= 2
    seg = x
    lane_in = lane & (chunk - 1)
    sh = 1
    while sh < chunk:
        seg = seg + jnp.where(lane_in >= sh, pltpu.roll(seg, sh, axis=1), 0.0)
        sh *= 2
    o_ref[...] = jnp.where(row < 8, glob, jnp.where(row < 12, x, seg))


def _scan_call(gates, *, chunk):
    B, R, S = gates.shape
    spec = pl.BlockSpec((None, R, S), lambda b: (b, 0, 0))
    return pl.pallas_call(
        functools.partial(_scan_kernel, chunk=chunk),
        grid=(B,), in_specs=[spec], out_specs=spec,
        out_shape=jax.ShapeDtypeStruct(gates.shape, F32),
        compiler_params=pltpu.CompilerParams(dimension_semantics=("arbitrary",)),
        name="gate_scan",
    )(gates)


def _split3(x):
    hi = x.astype(BF16).astype(F32)
    r = x - hi
    mid = r.astype(BF16).astype(F32)
    lo = (r - mid).astype(BF16).astype(F32)
    return hi, mid, lo


def _fox_kernel(cs_ref, q_ref, k_ref, v_ref, c_ref, o_ref,
                kaug_ref, vtaug_ref, qt_ref, acc_ref, m_ref, s0_ref, s1_ref, p0_ref, p1_ref, *, t, nk, heads):
    b = pl.program_id(0)
    i = pl.program_id(1)
    cs_base = b * heads * nk
    half = FOX_HEAD_DIM
    pairs = heads // 2

    def cs(h, j):
        return cs_ref[cs_base + h * nk + j]

    bound = cs_ref[pl.num_programs(0) * heads * nk]
    safe = 2.0 * bound < SAFE_EXP2_RANGE

    row = lax.broadcasted_iota(jnp.int32, (LANES, t), 0)
    top = row < half
    lane = lax.broadcasted_iota(jnp.int32, (1, LANES), 1)
    lo_mask = lane < half
    row8 = lax.broadcasted_iota(jnp.int32, (8, t), 0)
    ones_a = jnp.where(row < half + 3, 1.0, 0.0)
    ones_b = jnp.where(row < 3, 1.0, 0.0)
    pad = jnp.zeros((half - 8, t), F32)

    def causal_mask(s):
        kr = lax.broadcasted_iota(jnp.int32, (t, t), 0)
        qc = lax.broadcasted_iota(jnp.int32, (t, t), 1)
        return jnp.where(kr <= qc, s, NEG_BIG)

    def shift_row(h, j):
        return (cs(h, j) - c_ref[h, pl.ds(i, 1), :]) * LOG2E + bound

    def prep_pair(pp):
        lanes = slice(pp * LANES, (pp + 1) * LANES)
        q_t = q_ref[:, lanes].astype(F32).T
        qs = (jnp.where(top, q_t, ones_a).astype(BF16), jnp.where(top, ones_b, q_t).astype(BF16))
        pieces = []
        for e in range(2):
            h = 2 * pp + e
            qt_ref[h] = qs[e]
            brel = (cs(h, i) - c_ref[h, pl.ds(i, 1), :]) * LOG2E
            hi, mid, lo = _split3(brel)
            pieces.append(jnp.where(row8 == 0, hi, jnp.where(row8 == 1, mid, jnp.where(row8 == 2, lo, 0.0))))
        extra_t = jnp.concatenate([pieces[1], pad, pieces[0], pad], axis=0).T.astype(BF16)
        k2 = k_ref[:, lanes]
        ks = (jnp.where(lo_mask, k2, extra_t), jnp.where(lo_mask, extra_t, k2))
        v_t = v_ref[:, lanes].astype(F32).T.astype(BF16)
        one = jnp.ones_like(v_t)
        vs = (jnp.where(top, v_t, one), jnp.where(top, one, v_t))
        for e in range(2):
            kaug_ref[2 * pp + e, i] = ks[e]
            vtaug_ref[2 * pp + e, i] = vs[e]
        return qs, ks, vs

    @pl.when(safe)
    def _():
        hq = t // 2
        lo, hi = slice(0, hq), slice(hq, t)
        kr = lax.broadcasted_iota(jnp.int32, (hq, hq), 0)
        qc = lax.broadcasted_iota(jnp.int32, (hq, hq), 1)
        tri = kr <= qc
        dot = functools.partial(jnp.dot, preferred_element_type=F32)

        def diag_scores(pp):
            qs, ks, vs = prep_pair(pp)
            out = []
            for e in range(2):
                s00 = jnp.where(tri, dot(ks[e][lo], qs[e][:, lo]), NEG_BIG)
                s01 = dot(ks[e][lo], qs[e][:, hi])
                s11 = jnp.where(tri, dot(ks[e][hi], qs[e][:, hi]), NEG_BIG)
                out.append((s00, s01, s11, vs[e]))
            return out

        def diag_values(pp, scores):
            for e, (s00, s01, s11, v_e) in enumerate(scores):
                h = 2 * pp + e
                shift = shift_row(h, i)
                p00 = jnp.exp2(s00 - shift[:, lo]).astype(BF16)
                p_hi = jnp.concatenate([jnp.exp2(s01 - shift[:, hi]), jnp.exp2(s11 - shift[:, hi])],
                                       axis=0).astype(BF16)
                acc_ref[h, :, lo] = dot(v_e[:, lo], p00)
                acc_ref[h, :, hi] = dot(v_e, p_hi)

        pending = diag_scores(0)
        for pp in range(1, pairs):
            nxt = diag_scores(pp)
            diag_values(pp - 1, pending)
            pending = nxt
        diag_values(pairs - 1, pending)

        def pexp(j, p_ref):
            for h in range(heads):
                s = jnp.dot(kaug_ref[h, j], qt_ref[h], preferred_element_type=F32)
                p_ref[h] = jnp.exp2(s - shift_row(h, j)).astype(BF16)

        def pv(j, p_ref):
            for h in range(heads):
                acc_ref[h] += jnp.dot(vtaug_ref[h, j], p_ref[h], preferred_element_type=F32)

        @pl.when(i >= 1)
        def _():
            pexp(0, p0_ref)

            def pair_body(n, carry):
                j = 2 * n
                pexp(j + 1, p1_ref)
                pv(j, p0_ref)
                pexp(j + 2, p0_ref)
                pv(j + 1, p1_ref)
                return carry

            lax.fori_loop(0, (i - 1) // 2, pair_body, 0)

            @pl.when(i % 2 == 1)
            def _():
                pv(i - 1, p0_ref)

            @pl.when(i % 2 == 0)
            def _():
                pexp(i - 1, p1_ref)
                pv(i - 2, p0_ref)
                pv(i - 1, p1_ref)

    @pl.when(jnp.logical_not(safe))
    def _():
        acc_ref[...] = jnp.zeros(acc_ref.shape, F32)
        for pp in range(pairs):
            prep_pair(pp)

        def one_pair(pp, carry):
            def scores(j, s_ref):
                for e in range(2):
                    s_ref[e] = jnp.dot(kaug_ref[2 * pp + e, j], qt_ref[2 * pp + e], preferred_element_type=F32)

            def consume(j, s_ref, masked):
                for e in range(2):
                    h = 2 * pp + e
                    s = s_ref[e]
                    if masked:
                        s = causal_mask(s)
                    d = (cs(h, j) - cs(h, i)) * LOG2E
                    m_old = m_ref[e]
                    m_new = jnp.maximum(m_old, jnp.max(s, axis=0, keepdims=True) - d)
                    alpha = jnp.exp2(m_old - m_new)
                    p = jnp.exp2(s - (m_new + d))
                    acc_ref[h] = alpha * acc_ref[h] + jnp.dot(vtaug_ref[h, j], p.astype(BF16),
                                                             preferred_element_type=F32)
                    m_ref[e] = m_new

            m_ref[...] = jnp.full(m_ref.shape, NEG_BIG, F32)
            scores(0, s0_ref)

            def pair_body(n, c2):
                j = 2 * n
                scores(j + 1, s1_ref)
                consume(j, s0_ref, False)
                scores(j + 2, s0_ref)
                consume(j + 1, s1_ref, False)
                return c2

            lax.fori_loop(0, i // 2, pair_body, 0)

            @pl.when(i % 2 == 0)
            def _():
                consume(i, s0_ref, True)

            @pl.when(i % 2 == 1)
            def _():
                scores(i, s1_ref)
                consume(i - 1, s0_ref, False)
                consume(i, s1_ref, True)

            return carry

        lax.fori_loop(0, pairs, one_pair, 0)

    for pp in range(pairs):
        a0 = acc_ref[2 * pp]
        a1 = acc_ref[2 * pp + 1]
        out_t = jnp.concatenate([a0[:half] / a0[half:], a1[half:] / a1[:half]], axis=0)
        o_ref[:, pp * LANES:(pp + 1) * LANES] = out_t.T.astype(o_ref.dtype)


def _fox_call(qa, ka, va, c4, cstart, *, t):
    B, S, W = qa.shape
    heads = W // FOX_HEAD_DIM
    nk = S // t
    tile = pl.BlockSpec((None, t, W), lambda b, i: (b, i, 0))
    cspec = pl.BlockSpec((None, heads, nk, t), lambda b, i: (b, 0, 0, 0))
    return pl.pallas_call(
        functools.partial(_fox_kernel, t=t, nk=nk, heads=heads),
        grid=(B, nk),
        in_specs=[pl.BlockSpec(memory_space=pltpu.SMEM), tile, tile, tile, cspec],
        out_specs=tile,
        out_shape=jax.ShapeDtypeStruct((B, S, W), BF16),
        scratch_shapes=[pltpu.VMEM((heads, nk, t, LANES), BF16), pltpu.VMEM((heads, nk, LANES, t), BF16),
                        pltpu.VMEM((heads, LANES, t), BF16), pltpu.VMEM((heads, LANES, t), F32),
                        pltpu.VMEM((2, 1, t), F32),
                        pltpu.VMEM((2, t, t), F32), pltpu.VMEM((2, t, t), F32),
                        pltpu.VMEM((heads, t, t), BF16), pltpu.VMEM((heads, t, t), BF16)],
        compiler_params=pltpu.CompilerParams(
            dimension_semantics=("arbitrary", "arbitrary"), vmem_limit_bytes=VMEM_LIMIT_BYTES),
        name="fox_attn",
    )(cstart, qa, ka, va, c4)


def _mlstm_stages(q_ref, kt_ref, v_ref, og_ref, b_ref, li_ref, gh_ref, c_ref, m_ref, out_ref, tile, fresh, *,
                  L, heads, nchunk):
    d = LANES
    r = lax.broadcasted_iota(jnp.int32, (L, L), 0)
    c = lax.broadcasted_iota(jnp.int32, (L, L), 1)
    causal = c <= r
    eye = c == r
    ones = jnp.ones((L, d), BF16)
    items = [(cc, hh) for cc in range(nchunk) for hh in range(heads)]
    col = lambda hh: slice(hh * d, (hh + 1) * d)
    rows = lambda cc: slice(cc * L, (cc + 1) * L)
    q_of = lambda it: q_ref[rows(it[0]), col(it[1])]
    kt_of = lambda it: kt_ref[col(it[1]), rows(it[0])]
    v_aug_of = lambda it: jnp.concatenate([v_ref[rows(it[0]), col(it[1])], ones], axis=1)
    st = {}

    def stage_scores():
        st["s_raw"] = {it: jnp.dot(q_of(it), kt_of(it), preferred_element_type=F32) for it in items}

    def stage_updates():
        b_row, u_row = {}, {}
        for cc, hh in items:
            ci = tile * nchunk + cc
            b_row[cc, hh] = b_ref[hh, pl.ds(ci, 1), :]
            u_row[cc, hh] = li_ref[hh, pl.ds(ci, 1), :] - b_row[cc, hh]
        st["b_row"], st["u_row"] = b_row, u_row
        st["b_last"] = {it: b_row[it][:, L - 1:L] for it in items}
        u_max = {it: jnp.max(u_row[it], axis=1, keepdims=True) for it in items}
        st["m_loc"] = {it: st["b_last"][it] + u_max[it] for it in items}
        kw = {it: (kt_of(it).astype(F32) * _exp(u_row[it] - u_max[it])).astype(BF16) for it in items}
        st["upd"] = {it: jnp.dot(kw[it], v_aug_of(it), preferred_element_type=F32) for it in items}

    def stage_intra():
        b_col = {it: jnp.sum(jnp.where(eye, st["b_row"][it], 0.0), axis=1, keepdims=True) for it in items}
        dmat = {it: jnp.where(causal, b_col[it] + st["u_row"][it], NEG_BIG) for it in items}
        st["b_col"] = b_col
        st["m_d"] = {it: jnp.max(dmat[it], axis=1, keepdims=True) for it in items}
        s = {it: (st["s_raw"][it] * _exp(dmat[it] - st["m_d"][it])).astype(BF16) for it in items}
        st["intra"] = {it: jnp.dot(s[it], v_aug_of(it), preferred_element_type=F32) for it in items}

    def stage_inter():
        c_state = [jnp.where(fresh, 0.0, c_ref[hh]) for hh in range(heads)]
        m_prev = [jnp.where(fresh, 0.0, m_ref[hh]) for hh in range(heads)]
        st["m_in"], c_in = {}, {}
        for cc, hh in items:
            st["m_in"][cc, hh] = m_prev[hh]
            c_in[cc, hh] = c_state[hh]
            m_new = jnp.maximum(st["b_last"][cc, hh] + m_prev[hh], st["m_loc"][cc, hh])
            decay = _exp(st["b_last"][cc, hh] + m_prev[hh] - m_new)
            gain = _exp(st["m_loc"][cc, hh] - m_new)
            c_state[hh] = jnp.tile(decay, (1, 2)) * c_state[hh] + jnp.tile(gain, (1, 2)) * st["upd"][cc, hh]
            m_prev[hh] = m_new
        for hh in range(heads):
            c_ref[hh] = c_state[hh]
            m_ref[hh] = m_prev[hh]
        st["inter"] = {it: jnp.dot(q_of(it), c_in[it].astype(BF16), preferred_element_type=F32)
                       for it in items}

    def stage_output():
        for cc, hh in items:
            a = st["b_col"][cc, hh] + st["m_in"][cc, hh]
            m_t = jnp.maximum(a, st["m_d"][cc, hh])
            w_inter = _exp(a - m_t)
            w_intra = _exp(st["m_d"][cc, hh] - m_t)
            inter, intra = st["inter"][cc, hh], st["intra"][cc, hh]
            num = w_inter * inter[:, :d] + w_intra * intra[:, :d]
            den = w_inter * inter[:, d:] + w_intra * intra[:, d:]
            hval = num / jnp.maximum(jnp.abs(den), _exp(-m_t))
            mu = jnp.mean(hval, axis=-1, keepdims=True)
            xc = hval - mu
            yn = xc * lax.rsqrt(jnp.mean(xc * xc, axis=-1, keepdims=True) + EPS)
            out_ref[rows(cc), col(hh)] = (yn * gh_ref[:, col(hh)]
                                          * og_ref[rows(cc), col(hh)].astype(F32)).astype(out_ref.dtype)

    return [stage_scores, stage_updates, stage_intra, stage_inter, stage_output]


def _tail_kernel(x_ref, ya_ref, mga_ref, mgb_ref, q0_ref, kt0_ref, v0_ref, og0_ref,
                 qn_ref, ktn_ref, vn_ref, ogn_ref, b_ref, li_ref, gh_ref,
                 wfo_ref, wmo_ref, wo_ref, gffn_ref, wg_ref, wu_ref, wd_ref, o_ref,
                 c_ref, m_ref, yb_ref, *, L, heads, nchunk):
    b = pl.program_id(0)
    i = pl.program_id(1)
    row_end = i == pl.num_programs(1) - 1
    cur = (b * pl.num_programs(1) + i) % 2
    kw = dict(L=L, heads=heads, nchunk=nchunk)

    @pl.when((b == 0) & (i == 0))
    def _():
        for stage in _mlstm_stages(q0_ref, kt0_ref, v0_ref, og0_ref, b_ref, li_ref, gh_ref, c_ref, m_ref,
                                   yb_ref.at[0], 0, True, **kw):
            stage()

    stages = _mlstm_stages(qn_ref, ktn_ref, vn_ref, ogn_ref, b_ref, li_ref, gh_ref, c_ref, m_ref,
                           yb_ref.at[1 - cur], jnp.where(row_end, 0, i + 1), row_end, **kw)

    pa = jnp.dot(ya_ref[...], wfo_ref[...], preferred_element_type=F32)
    stages[0]()
    pb = jnp.dot(yb_ref[cur], wmo_ref[...], preferred_element_type=F32)
    stages[1]()
    merged = mga_ref[...].astype(F32) * pa + mgb_ref[...].astype(F32) * pb
    x1 = x_ref[...] + jnp.dot(merged.astype(BF16), wo_ref[...], preferred_element_type=F32)
    stages[2]()
    h2 = (x1 * lax.rsqrt(jnp.mean(x1 * x1, axis=-1, keepdims=True) + EPS) * gffn_ref[...]).astype(BF16)
    d_ff = wg_ref.shape[1]
    split = (d_ff // 2 + MXU_TILE - 1) // MXU_TILE * MXU_TILE
    acts = []
    for cols, after_gate in ((slice(0, split), stages[3]), (slice(split, d_ff), stages[4])):
        gate = jnp.dot(h2, wg_ref[:, cols], preferred_element_type=F32)
        after_gate()
        up = jnp.dot(h2, wu_ref[:, cols], preferred_element_type=F32)
        acts.append((gate * _sigmoid(gate) * up).astype(BF16))
    act = jnp.concatenate(acts, axis=1)
    o_ref[...] = x1 + jnp.dot(act, wd_ref[...], preferred_element_type=F32)


def _tail_call(x, ya, mga, mgb, qb, kbt, vb, og, bcs, li, gh, wfo, wmo, wo, gffn, wg, wu, wd, *, tm, L):
    B, S, D = x.shape
    W = qb.shape[2]
    heads = W // LANES
    nt = S // tm
    tok = lambda width: pl.BlockSpec((None, tm, width), lambda b, i: (b, i, 0))
    first = pl.BlockSpec((None, tm, W), lambda b, i: (0, 0, 0), pipeline_mode=pl.Buffered(1))
    first_t = pl.BlockSpec((None, W, tm), lambda b, i: (0, 0, 0), pipeline_mode=pl.Buffered(1))
    n_row = lambda b, i: jnp.where(i == nt - 1, jnp.minimum(b + 1, B - 1), b)
    n_tile = lambda b, i: jnp.where(i == nt - 1, 0, i + 1)
    nxt = pl.BlockSpec((None, tm, W), lambda b, i: (n_row(b, i), n_tile(b, i), 0))
    nxt_t = pl.BlockSpec((None, W, tm), lambda b, i: (n_row(b, i), 0, n_tile(b, i)))
    gspec = pl.BlockSpec((None, heads, S // L, L), lambda b, i: (n_row(b, i), 0, 0, 0))
    in_specs = [tok(D), tok(ya.shape[2]), tok(D), tok(D),
                first, first_t, first, first, nxt, nxt_t, nxt, nxt, gspec, gspec, _const_spec(gh.shape),
                _const_spec(wfo.shape), _const_spec(wmo.shape), _const_spec(wo.shape), _const_spec(gffn.shape),
                _const_spec(wg.shape), _const_spec(wu.shape), _const_spec(wd.shape)]
    return pl.pallas_call(
        functools.partial(_tail_kernel, L=L, heads=heads, nchunk=tm // L),
        grid=(B, nt), in_specs=in_specs, out_specs=tok(D),
        out_shape=jax.ShapeDtypeStruct((B, S, D), x.dtype),
        scratch_shapes=[pltpu.VMEM((heads, LANES, 2 * LANES), F32), pltpu.VMEM((heads, 1, LANES), F32),
                        pltpu.VMEM((2, tm, W), BF16)],
        compiler_params=pltpu.CompilerParams(
            dimension_semantics=("arbitrary", "arbitrary"), vmem_limit_bytes=TAIL_VMEM_LIMIT_BYTES),
        name="tail",
    )(x, ya, mga, mgb, qb, kbt, vb, og, qb, kbt, vb, og, bcs, li, gh, wfo, wmo, wo, gffn, wg, wu, wd)


def _block(x, g_mix, w_in, b_fox_f, g_q_fox, g_k_fox, conv_w, conv_b, b_mlstm_i, b_mlstm_f,
           g_mlstm_h, w_fox_out, w_mlstm_out, w_o, g_ffn, w_gate, w_up, w_down):
    B, S, D = x.shape
    fw = w_fox_out.shape[0]
    mw = w_mlstm_out.shape[0]
    fh = fw // FOX_HEAD_DIM
    assert fh == 8 and mw // LANES == MLSTM_HEADS and fw % LANES == 0
    tm = min(512, S)
    t_attn = min(512, S)
    chunk = min(128, S)
    assert S % tm == 0 and S % t_attn == 0 and tm % chunk == 0

    sizes = (fw, fw, fw, fh, mw, mw, mw, MLSTM_HEADS, MLSTM_HEADS, mw, D, D)
    offs = [0]
    for sz in sizes:
        offs.append(offs[-1] + sz)
    col = lambda k: w_in[:, offs[k]:offs[k + 1]]
    w_main = jnp.concatenate([col(0), col(1), col(2), col(4), col(5), col(6), col(9), col(10), col(11)],
                             axis=1).astype(BF16)
    w_gt = jnp.concatenate([col(3), col(7), col(8)], axis=1).T.astype(BF16)
    gate_bias = jnp.concatenate([b_fox_f, b_mlstm_i, b_mlstm_f]).astype(F32)[:, None]
    root_d = FOX_HEAD_DIM ** 0.5
    gq = jnp.tile(g_q_fox.astype(F32) * (root_d * FOX_HEAD_DIM ** -0.5 * LOG2E), fh)[None, :]
    gk = jnp.tile(g_k_fox.astype(F32) * root_d, fh)[None, :]

    (qa, ka, va, qb, kbt, vb, og, mga, mgb, gates) = _proj_call(
        x, g_mix.astype(F32)[None, :], w_main, w_gt, gate_bias, gq, gk,
        conv_w.astype(F32), conv_b.astype(F32)[None, :], tm=tm, nsub=2 if tm % 512 == 0 else 1)

    scans = _scan_call(gates, chunk=chunk)
    c4 = scans[:, :fh].reshape(B, fh, S // t_attn, t_attn)
    li = scans[:, fh:fh + MLSTM_HEADS].reshape(B, MLSTM_HEADS, S // chunk, chunk)
    bcs = scans[:, fh + MLSTM_HEADS:].reshape(B, MLSTM_HEADS, S // chunk, chunk)

    qk_bound = (FOX_HEAD_DIM * (FOX_HEAD_DIM ** -0.5 * LOG2E) * 1.02
                * jnp.max(jnp.abs(g_q_fox.astype(F32))) * jnp.max(jnp.abs(g_k_fox.astype(F32))))
    cstart = jnp.concatenate([c4[..., 0].reshape(-1), qk_bound[None]])
    ya = _fox_call(qa, ka, va, c4, cstart, t=t_attn)
    return _tail_call(x, ya, mga, mgb, qb, kbt, vb, og, bcs, li, g_mlstm_h.astype(F32)[None, :],
                      w_fox_out.astype(BF16), w_mlstm_out.astype(BF16), w_o.astype(BF16),
                      g_ffn.astype(F32)[None, :], w_gate.astype(BF16), w_up.astype(BF16),
                      w_down.astype(BF16), tm=tm, L=chunk)


def kernel(x, g_mix, w_in, b_fox_f, g_q_fox, g_k_fox, conv_w, conv_b, b_mlstm_i, b_mlstm_f, g_mlstm_h,
           w_fox_out, w_mlstm_out, w_o, g_ffn, w_gate, w_up, w_down):
    for l in range(g_mix.shape[0]):
        x = _block(x, g_mix[l], w_in[l], b_fox_f[l], g_q_fox[l], g_k_fox[l], conv_w[l], conv_b[l],
                   b_mlstm_i[l], b_mlstm_f[l], g_mlstm_h[l], w_fox_out[l], w_mlstm_out[l], w_o[l],
                   g_ffn[l], w_gate[l], w_up[l], w_down[l])
    return x
```

```python
import functools

import jax
import jax.numpy as jnp
from jax import lax
from jax.experimental import pallas as pl
from jax.experimental.pallas import tpu as pltpu

EPS = 1e-6
FOX_HEAD_DIM = 64
MLSTM_HEADS = 4
CONV_WIDTH = 4
LANES = 128
MXU_TILE = 256
NEG_BIG = -1e30
LOG2E = 1.4426950408889634
SAFE_EXP2_RANGE = 100.0
VMEM_LIMIT_BYTES = 56 * 1024 * 1024
TAIL_VMEM_LIMIT_BYTES = 60 * 1024 * 1024

F32 = jnp.float32
BF16 = jnp.bfloat16


def _const_spec(shape):
    nd = len(shape)
    return pl.BlockSpec(shape, lambda *_: (0,) * nd, pipeline_mode=pl.Buffered(1))


def _exp(x):
    return jnp.exp2(x * LOG2E)


def _log_sigmoid(z):
    return jnp.minimum(z, 0.0) - jnp.log(1.0 + _exp(-jnp.abs(z)))


def _sigmoid(z):
    return 1.0 / (1.0 + jnp.exp2(z * (-LOG2E)))


def _head_rms(x, g_row, lo_mask):
    outs = []
    for j in range(x.shape[1] // LANES):
        v = x[:, j * LANES:(j + 1) * LANES]
        v2 = v * v
        s_lo = jnp.sum(jnp.where(lo_mask, v2, 0.0), axis=-1, keepdims=True)
        s_hi = jnp.sum(jnp.where(lo_mask, 0.0, v2), axis=-1, keepdims=True)
        outs.append(v * lax.rsqrt(jnp.where(lo_mask, s_lo, s_hi) + FOX_HEAD_DIM * EPS))
    return jnp.concatenate(outs, axis=-1) * g_row


def _proj_kernel(x_ref, gmix_ref, w_ref, wg_ref, gb_ref, gq_ref, gk_ref, cw_ref, cb_ref,
                 qa_ref, ka_ref, va_ref, qb_ref, kbt_ref, vb_ref, og_ref, mga_ref, mgb_ref, gates_ref,
                 cbuf_ref, *, tm, nsub, fw, mw, d_model, k_scale):
    i = pl.program_id(1)
    lane = lax.broadcasted_iota(jnp.int32, (1, LANES), 1)
    lo_mask = lane < FOX_HEAD_DIM

    @pl.when(i == 0)
    def _():
        cbuf_ref[0:8, :] = jnp.zeros((8, 2 * mw), F32)

    ts = tm // nsub
    for r in range(nsub):
        rows = slice(r * ts, (r + 1) * ts)
        c0 = 8 + r * ts
        x = x_ref[rows, :]
        h = x * lax.rsqrt(jnp.mean(x * x, axis=-1, keepdims=True) + EPS) * gmix_ref[...]
        hb = h.astype(BF16)

        def proj(lo, width):
            return jnp.dot(hb, w_ref[:, lo:lo + width], preferred_element_type=F32)

        o_qb = 3 * fw
        cbuf_ref[c0:c0 + ts, 0:mw] = proj(o_qb, mw)
        cbuf_ref[c0:c0 + ts, mw:2 * mw] = proj(o_qb + mw, mw)
        qa_ref[rows, :] = _head_rms(proj(0, fw), gq_ref[...], lo_mask).astype(BF16)
        ka_ref[rows, :] = _head_rms(proj(fw, fw), gk_ref[...], lo_mask).astype(BF16)
        va_ref[rows, :] = proj(2 * fw, fw).astype(BF16)
        vb_ref[rows, :] = proj(o_qb + 2 * mw, mw).astype(BF16)
        og_ref[rows, :] = _sigmoid(proj(o_qb + 3 * mw, mw)).astype(BF16)
        mga_ref[rows, :] = _sigmoid(proj(o_qb + 4 * mw, d_model)).astype(BF16)
        mgb_ref[rows, :] = _sigmoid(proj(o_qb + 4 * mw + d_model, d_model)).astype(BF16)

        z = lax.dot_general(wg_ref[...], hb, (((1,), (1,)), ((), ())), preferred_element_type=F32) + gb_ref[...]
        row = lax.broadcasted_iota(jnp.int32, z.shape, 0)
        is_input_gate = (row >= 8) & (row < 12)
        gates_ref[:, rows] = jnp.where(is_input_gate, z, _log_sigmoid(z))

        for half in range(2):
            cols = slice(half * mw, (half + 1) * mw)
            y = cb_ref[:, cols] + cw_ref[CONV_WIDTH - 1:CONV_WIDTH, cols] * cbuf_ref[c0:c0 + ts, cols]
            for j in range(CONV_WIDTH - 1):
                r0 = c0 - (CONV_WIDTH - 1) + j
                y = y + cw_ref[j:j + 1, cols] * cbuf_ref[r0:r0 + ts, cols]
            y = y * _sigmoid(y)
            if half == 0:
                qb_ref[rows, :] = y.astype(BF16)
            else:
                kbt_ref[:, rows] = (y * k_scale).T.astype(BF16)
    cbuf_ref[0:8, :] = cbuf_ref[tm:tm + 8, :]


def _proj_call(x, gmix, w_main, w_gt, gate_bias, gq, gk, conv_w, conv_b, *, tm, nsub):
    B, S, D = x.shape
    fw = gq.shape[1]
    mw = conv_w.shape[1] // 2
    n_main = w_main.shape[1]
    grid = (B, S // tm)
    tok = lambda width: pl.BlockSpec((None, tm, width), lambda b, i: (b, i, 0))
    out_shape = (
        jax.ShapeDtypeStruct((B, S, fw), BF16),
        jax.ShapeDtypeStruct((B, S, fw), BF16),
        jax.ShapeDtypeStruct((B, S, fw), BF16),
        jax.ShapeDtypeStruct((B, S, mw), BF16),
        jax.ShapeDtypeStruct((B, mw, S), BF16),
        jax.ShapeDtypeStruct((B, S, mw), BF16),
        jax.ShapeDtypeStruct((B, S, mw), BF16),
        jax.ShapeDtypeStruct((B, S, D), BF16),
        jax.ShapeDtypeStruct((B, S, D), BF16),
        jax.ShapeDtypeStruct((B, 16, S), F32),
    )
    out_specs = (
        tok(fw), tok(fw), tok(fw), tok(mw),
        pl.BlockSpec((None, mw, tm), lambda b, i: (b, 0, i)),
        tok(mw), tok(mw), tok(D), tok(D),
        pl.BlockSpec((None, 16, tm), lambda b, i: (b, 0, i)),
    )
    in_specs = [
        tok(D),
        _const_spec((1, D)),
        _const_spec((D, n_main)),
        _const_spec((16, D)),
        _const_spec((16, 1)),
        _const_spec((1, fw)),
        _const_spec((1, fw)),
        _const_spec((CONV_WIDTH, 2 * mw)),
        _const_spec((1, 2 * mw)),
    ]
    kern = functools.partial(_proj_kernel, tm=tm, nsub=nsub, fw=fw, mw=mw, d_model=D,
                             k_scale=(mw // MLSTM_HEADS) ** -0.5)
    return pl.pallas_call(
        kern, grid=grid, in_specs=in_specs, out_specs=out_specs, out_shape=out_shape,
        scratch_shapes=[pltpu.VMEM((tm + 8, 2 * mw), F32)],
        compiler_params=pltpu.CompilerParams(
            dimension_semantics=("arbitrary", "arbitrary"), vmem_limit_bytes=VMEM_LIMIT_BYTES),
        name="proj",
    )(x, gmix, w_main, w_gt, gate_bias, gq, gk, conv_w, conv_b)


def _scan_kernel(g_ref, o_ref, *, chunk):
    x = g_ref[...]
    S = x.shape[1]
    lane = lax.broadcasted_iota(jnp.int32, x.shape, 1)
    row = lax.broadcasted_iota(jnp.int32, x.shape, 0)
    glob = x
    sh = 1
    while sh < S:
        glob = glob + jnp.where(lane >= sh, pltpu.roll(glob, sh, axis=1), 0.0)
        sh *= 2
    seg = x
    lane_in = lane & (chunk - 1)
    sh = 1
    while sh < chunk:
        seg = seg + jnp.where(lane_in >= sh, pltpu.roll(seg, sh, axis=1), 0.0)
        sh *= 2
    o_ref[...] = jnp.where(row < 8, glob, jnp.where(row < 12, x, seg))


def _scan_call(gates, *, chunk):
    B, R, S = gates.shape
    spec = pl.BlockSpec((None, R, S), lambda b: (b, 0, 0))
    return pl.pallas_call(
        functools.partial(_scan_kernel, chunk=chunk),
        grid=(B,), in_specs=[spec], out_specs=spec,
        out_shape=jax.ShapeDtypeStruct(gates.shape, F32),
        compiler_params=pltpu.CompilerParams(dimension_semantics=("arbitrary",)),
        name="gate_scan",
    )(gates)


def _split3(x):
    hi = x.astype(BF16).astype(F32)
    r = x - hi
    mid = r.astype(BF16).astype(F32)
    lo = (r - mid).astype(BF16).astype(F32)
    return hi, mid, lo


def _fox_kernel(cs_ref, q_ref, k_ref, v_ref, c_ref, o_ref,
                kaug_ref, vtaug_ref, qt_ref, acc_ref, m_ref, s0_ref, s1_ref, p0_ref, p1_ref, *, t, nk, heads):
    b = pl.program_id(0)
    i = pl.program_id(1)
    cs_base = b * heads * nk
    half = FOX_HEAD_DIM
    pairs = heads // 2

    def cs(h, j):
        return cs_ref[cs_base + h * nk + j]

    bound = cs_ref[pl.num_programs(0) * heads * nk]
    safe = 2.0 * bound < SAFE_EXP2_RANGE

    row = lax.broadcasted_iota(jnp.int32, (LANES, t), 0)
    top = row < half
    lane = lax.broadcasted_iota(jnp.int32, (1, LANES), 1)
    lo_mask = lane < half
    row8 = lax.broadcasted_iota(jnp.int32, (8, t), 0)
    ones_a = jnp.where(row < half + 3, 1.0, 0.0)
    ones_b = jnp.where(row < 3, 1.0, 0.0)
    pad = jnp.zeros((half - 8, t), F32)

    def causal_mask(s):
        kr = lax.broadcasted_iota(jnp.int32, (t, t), 0)
        qc = lax.broadcasted_iota(jnp.int32, (t, t), 1)
        return jnp.where(kr <= qc, s, NEG_BIG)

    def shift_row(h, j):
        return (cs(h, j) - c_ref[h, pl.ds(i, 1), :]) * LOG2E + bound

    def prep_pair(pp):
        lanes = slice(pp * LANES, (pp + 1) * LANES)
        q_t = q_ref[:, lanes].astype(F32).T
        qs = (jnp.where(top, q_t, ones_a).astype(BF16), jnp.where(top, ones_b, q_t).astype(BF16))
        pieces = []
        for e in range(2):
            h = 2 * pp + e
            qt_ref[h] = qs[e]
            brel = (cs(h, i) - c_ref[h, pl.ds(i, 1), :]) * LOG2E
            hi, mid, lo = _split3(brel)
            pieces.append(jnp.where(row8 == 0, hi, jnp.where(row8 == 1, mid, jnp.where(row8 == 2, lo, 0.0))))
        extra_t = jnp.concatenate([pieces[1], pad, pieces[0], pad], axis=0).T.astype(BF16)
        k2 = k_ref[:, lanes]
        ks = (jnp.where(lo_mask, k2, extra_t), jnp.where(lo_mask, extra_t, k2))
        v_t = v_ref[:, lanes].astype(F32).T.astype(BF16)
        one = jnp.ones_like(v_t)
        vs = (jnp.where(top, v_t, one), jnp.where(top, one, v_t))
        for e in range(2):
            kaug_ref[2 * pp + e, i] = ks[e]
            vtaug_ref[2 * pp + e, i] = vs[e]
        return qs, ks, vs

    @pl.when(safe)
    def _():
        hq = t // 2
        lo, hi = slice(0, hq), slice(hq, t)
        kr = lax.broadcasted_iota(jnp.int32, (hq, hq), 0)
        qc = lax.broadcasted_iota(jnp.int32, (hq, hq), 1)
        tri = kr <= qc
        dot = functools.partial(jnp.dot, preferred_element_type=F32)

        def diag_scores(pp):
            qs, ks, vs = prep_pair(pp)
            out = []
            for e in range(2):
                s00 = jnp.where(tri, dot(ks[e][lo], qs[e][:, lo]), NEG_BIG)
                s01 = dot(ks[e][lo], qs[e][:, hi])
                s11 = jnp.where(tri, dot(ks[e][hi], qs[e][:, hi]), NEG_BIG)
                out.append((s00, s01, s11, vs[e]))
            return out

        def diag_values(pp, scores):
            for e, (s00, s01, s11, v_e) in enumerate(scores):
                h = 2 * pp + e
                shift = shift_row(h, i)
                p00 = jnp.exp2(s00 - shift[:, lo]).astype(BF16)
                p_hi = jnp.concatenate([jnp.exp2(s01 - shift[:, hi]), jnp.exp2(s11 - shift[:, hi])],
                                       axis=0).astype(BF16)
                acc_ref[h, :, lo] = dot(v_e[:, lo], p00)
                acc_ref[h, :, hi] = dot(v_e, p_hi)

        pending = diag_scores(0)
        for pp in range(1, pairs):
            nxt = diag_scores(pp)
            diag_values(pp - 1, pending)
            pending = nxt
        diag_values(pairs - 1, pending)

        def pexp(j, p_ref):
            for h in range(heads):
                s = jnp.dot(kaug_ref[h, j], qt_ref[h], preferred_element_type=F32)
                p_ref[h] = jnp.exp2(s - shift_row(h, j)).astype(BF16)

        def pv(j, p_ref):
            for h in range(heads):
                acc_ref[h] += jnp.dot(vtaug_ref[h, j], p_ref[h], preferred_element_type=F32)

        @pl.when(i >= 1)
        def _():
            pexp(0, p0_ref)

            def pair_body(n, carry):
                j = 2 * n
                pexp(j + 1, p1_ref)
                pv(j, p0_ref)
                pexp(j + 2, p0_ref)
                pv(j + 1, p1_ref)
                return carry

            lax.fori_loop(0, (i - 1) // 2, pair_body, 0)

            @pl.when(i % 2 == 1)
            def _():
                pv(i - 1, p0_ref)

            @pl.when(i % 2 == 0)
            def _():
                pexp(i - 1, p1_ref)
                pv(i - 2, p0_ref)
                pv(i - 1, p1_ref)

    @pl.when(jnp.logical_not(safe))
    def _():
        acc_ref[...] = jnp.zeros(acc_ref.shape, F32)
        for pp in range(pairs):
            prep_pair(pp)

        def one_pair(pp, carry):
            def scores(j, s_ref):
                for e in range(2):
                    s_ref[e] = jnp.dot(kaug_ref[2 * pp + e, j], qt_ref[2 * pp + e], preferred_element_type=F32)

            def consume(j, s_ref, masked):
                for e in range(2):
                    h = 2 * pp + e
                    s = s_ref[e]
                    if masked:
                        s = causal_mask(s)
                    d = (cs(h, j) - cs(h, i)) * LOG2E
                    m_old = m_ref[e]
                    m_new = jnp.maximum(m_old, jnp.max(s, axis=0, keepdims=True) - d)
                    alpha = jnp.exp2(m_old - m_new)
                    p = jnp.exp2(s - (m_new + d))
                    acc_ref[h] = alpha * acc_ref[h] + jnp.dot(vtaug_ref[h, j], p.astype(BF16),
                                                             preferred_element_type=F32)
                    m_ref[e] = m_new

            m_ref[...] = jnp.full(m_ref.shape, NEG_BIG, F32)
            scores(0, s0_ref)

            def pair_body(n, c2):
                j = 2 * n
                scores(j + 1, s1_ref)
                consume(j, s0_ref, False)
                scores(j + 2, s0_ref)
                consume(j + 1, s1_ref, False)
                return c2

            lax.fori_loop(0, i // 2, pair_body, 0)

            @pl.when(i % 2 == 0)
            def _():
                consume(i, s0_ref, True)

            @pl.when(i % 2 == 1)
            def _():
                scores(i, s1_ref)
                consume(i - 1, s0_ref, False)
                consume(i, s1_ref, True)

            return carry

        lax.fori_loop(0, pairs, one_pair, 0)

    for pp in range(pairs):
        a0 = acc_ref[2 * pp]
        a1 = acc_ref[2 * pp + 1]
        out_t = jnp.concatenate([a0[:half] / a0[half:], a1[half:] / a1[:half]], axis=0)
        o_ref[:, pp * LANES:(pp + 1) * LANES] = out_t.T.astype(o_ref.dtype)


def _fox_call(qa, ka, va, c4, cstart, *, t):
    B, S, W = qa.shape
    heads = W // FOX_HEAD_DIM
    nk = S // t
    tile = pl.BlockSpec((None, t, W), lambda b, i: (b, i, 0))
    cspec = pl.BlockSpec((None, heads, nk, t), lambda b, i: (b, 0, 0, 0))
    return pl.pallas_call(
        functools.partial(_fox_kernel, t=t, nk=nk, heads=heads),
        grid=(B, nk),
        in_specs=[pl.BlockSpec(memory_space=pltpu.SMEM), tile, tile, tile, cspec],
        out_specs=tile,
        out_shape=jax.ShapeDtypeStruct((B, S, W), BF16),
        scratch_shapes=[pltpu.VMEM((heads, nk, t, LANES), BF16), pltpu.VMEM((heads, nk, LANES, t), BF16),
                        pltpu.VMEM((heads, LANES, t), BF16), pltpu.VMEM((heads, LANES, t), F32),
                        pltpu.VMEM((2, 1, t), F32),
                        pltpu.VMEM((2, t, t), F32), pltpu.VMEM((2, t, t), F32),
                        pltpu.VMEM((heads, t, t), BF16), pltpu.VMEM((heads, t, t), BF16)],
        compiler_params=pltpu.CompilerParams(
            dimension_semantics=("arbitrary", "arbitrary"), vmem_limit_bytes=VMEM_LIMIT_BYTES),
        name="fox_attn",
    )(cstart, qa, ka, va, c4)


def _mlstm_stages(q_ref, kt_ref, v_ref, og_ref, b_ref, li_ref, gh_ref, c_ref, m_ref, out_ref, tile, fresh, *,
                  L, heads, nchunk):
    d = LANES
    r = lax.broadcasted_iota(jnp.int32, (L, L), 0)
    c = lax.broadcasted_iota(jnp.int32, (L, L), 1)
    causal = c <= r
    eye = c == r
    ones = jnp.ones((L, d), BF16)
    items = [(cc, hh) for cc in range(nchunk) for hh in range(heads)]
    col = lambda hh: slice(hh * d, (hh + 1) * d)
    rows = lambda cc: slice(cc * L, (cc + 1) * L)
    q_of = lambda it: q_ref[rows(it[0]), col(it[1])]
    kt_of = lambda it: kt_ref[col(it[1]), rows(it[0])]
    v_aug_of = lambda it: jnp.concatenate([v_ref[rows(it[0]), col(it[1])], ones], axis=1)
    st = {}

    def stage_scores():
        st["s_raw"] = {it: jnp.dot(q_of(it), kt_of(it), preferred_element_type=F32) for it in items}

    def stage_updates():
        b_row, u_row = {}, {}
        for cc, hh in items:
            ci = tile * nchunk + cc
            b_row[cc, hh] = b_ref[hh, pl.ds(ci, 1), :]
            u_row[cc, hh] = li_ref[hh, pl.ds(ci, 1), :] - b_row[cc, hh]
        st["b_row"], st["u_row"] = b_row, u_row
        st["b_last"] = {it: b_row[it][:, L - 1:L] for it in items}
        u_max = {it: jnp.max(u_row[it], axis=1, keepdims=True) for it in items}
        st["m_loc"] = {it: st["b_last"][it] + u_max[it] for it in items}
        kw = {it: (kt_of(it).astype(F32) * _exp(u_row[it] - u_max[it])).astype(BF16) for it in items}
        st["upd"] = {it: jnp.dot(kw[it], v_aug_of(it), preferred_element_type=F32) for it in items}

    def stage_intra():
        b_col = {it: jnp.sum(jnp.where(eye, st["b_row"][it], 0.0), axis=1, keepdims=True) for it in items}
        dmat = {it: jnp.where(causal, b_col[it] + st["u_row"][it], NEG_BIG) for it in items}
        st["b_col"] = b_col
        st["m_d"] = {it: jnp.max(dmat[it], axis=1, keepdims=True) for it in items}
        s = {it: (st["s_raw"][it] * _exp(dmat[it] - st["m_d"][it])).astype(BF16) for it in items}
        st["intra"] = {it: jnp.dot(s[it], v_aug_of(it), preferred_element_type=F32) for it in items}

    def stage_inter():
        c_state = [jnp.where(fresh, 0.0, c_ref[hh]) for hh in range(heads)]
        m_prev = [jnp.where(fresh, 0.0, m_ref[hh]) for hh in range(heads)]
        st["m_in"], c_in = {}, {}
        for cc, hh in items:
            st["m_in"][cc, hh] = m_prev[hh]
            c_in[cc, hh] = c_state[hh]
            m_new = jnp.maximum(st["b_last"][cc, hh] + m_prev[hh], st["m_loc"][cc, hh])
            decay = _exp(st["b_last"][cc, hh] + m_prev[hh] - m_new)
            gain = _exp(st["m_loc"][cc, hh] - m_new)
            c_state[hh] = jnp.tile(decay, (1, 2)) * c_state[hh] + jnp.tile(gain, (1, 2)) * st["upd"][cc, hh]
            m_prev[hh] = m_new
        for hh in range(heads):
            c_ref[hh] = c_state[hh]
            m_ref[hh] = m_prev[hh]
        st["inter"] = {it: jnp.dot(q_of(it), c_in[it].astype(BF16), preferred_element_type=F32)
                       for it in items}

    def stage_output():
        for cc, hh in items:
            a = st["b_col"][cc, hh] + st["m_in"][cc, hh]
            m_t = jnp.maximum(a, st["m_d"][cc, hh])
            w_inter = _exp(a - m_t)
            w_intra = _exp(st["m_d"][cc, hh] - m_t)
            inter, intra = st["inter"][cc, hh], st["intra"][cc, hh]
            num = w_inter * inter[:, :d] + w_intra * intra[:, :d]
            den = w_inter * inter[:, d:] + w_intra * intra[:, d:]
            hval = num / jnp.maximum(jnp.abs(den), _exp(-m_t))
            mu = jnp.mean(hval, axis=-1, keepdims=True)
            xc = hval - mu
            yn = xc * lax.rsqrt(jnp.mean(xc * xc, axis=-1, keepdims=True) + EPS)
            out_ref[rows(cc), col(hh)] = (yn * gh_ref[:, col(hh)]
                                          * og_ref[rows(cc), col(hh)].astype(F32)).astype(out_ref.dtype)

    return [stage_scores, stage_updates, stage_intra, stage_inter, stage_output]


def _tail_kernel(x_ref, ya_ref, mga_ref, mgb_ref, q0_ref, kt0_ref, v0_ref, og0_ref,
                 qn_ref, ktn_ref, vn_ref, ogn_ref, b_ref, li_ref, gh_ref,
                 wfo_ref, wmo_ref, wo_ref, gffn_ref, wg_ref, wu_ref, wd_ref, o_ref,
                 c_ref, m_ref, yb_ref, *, L, heads, nchunk):
    b = pl.program_id(0)
    i = pl.program_id(1)
    row_end = i == pl.num_programs(1) - 1
    cur = (b * pl.num_programs(1) + i) % 2
    kw = dict(L=L, heads=heads, nchunk=nchunk)

    @pl.when((b == 0) & (i == 0))
    def _():
        for stage in _mlstm_stages(q0_ref, kt0_ref, v0_ref, og0_ref, b_ref, li_ref, gh_ref, c_ref, m_ref,
                                   yb_ref.at[0], 0, True, **kw):
            stage()

    stages = _mlstm_stages(qn_ref, ktn_ref, vn_ref, ogn_ref, b_ref, li_ref, gh_ref, c_ref, m_ref,
                           yb_ref.at[1 - cur], jnp.where(row_end, 0, i + 1), row_end, **kw)

    pa = jnp.dot(ya_ref[...], wfo_ref[...], preferred_element_type=F32)
    stages[0]()
    pb = jnp.dot(yb_ref[cur], wmo_ref[...], preferred_element_type=F32)
    stages[1]()
    merged = mga_ref[...].astype(F32) * pa + mgb_ref[...].astype(F32) * pb
    x1 = x_ref[...] + jnp.dot(merged.astype(BF16), wo_ref[...], preferred_element_type=F32)
    stages[2]()
    h2 = (x1 * lax.rsqrt(jnp.mean(x1 * x1, axis=-1, keepdims=True) + EPS) * gffn_ref[...]).astype(BF16)
    d_ff = wg_ref.shape[1]
    slab = MXU_TILE
    bounds = [(lo, min(lo + slab, d_ff)) for lo in range(0, d_ff, slab)]
    after = {1: stages[3], len(bounds) // 2 + 1: stages[4]}
    o_ref[...] = x1
    for n, (lo, hi) in enumerate(bounds):
        gate = jnp.dot(h2, wg_ref[:, lo:hi], preferred_element_type=F32)
        up = jnp.dot(h2, wu_ref[:, lo:hi], preferred_element_type=F32)
        act = (gate * _sigmoid(gate) * up).astype(BF16)
        o_ref[...] += jnp.dot(act, wd_ref[lo:hi, :], preferred_element_type=F32)
        if n in after:
            after[n]()


def _tail_call(x, ya, mga, mgb, qb, kbt, vb, og, bcs, li, gh, wfo, wmo, wo, gffn, wg, wu, wd, *, tm, L):
    B, S, D = x.shape
    W = qb.shape[2]
    heads = W // LANES
    nt = S // tm
    tok = lambda width: pl.BlockSpec((None, tm, width), lambda b, i: (b, i, 0))
    first = pl.BlockSpec((None, tm, W), lambda b, i: (0, 0, 0), pipeline_mode=pl.Buffered(1))
    first_t = pl.BlockSpec((None, W, tm), lambda b, i: (0, 0, 0), pipeline_mode=pl.Buffered(1))
    n_row = lambda b, i: jnp.where(i == nt - 1, jnp.minimum(b + 1, B - 1), b)
    n_tile = lambda b, i: jnp.where(i == nt - 1, 0, i + 1)
    nxt = pl.BlockSpec((None, tm, W), lambda b, i: (n_row(b, i), n_tile(b, i), 0))
    nxt_t = pl.BlockSpec((None, W, tm), lambda b, i: (n_row(b, i), 0, n_tile(b, i)))
    gspec = pl.BlockSpec((None, heads, S // L, L), lambda b, i: (n_row(b, i), 0, 0, 0))
    in_specs = [tok(D), tok(ya.shape[2]), tok(D), tok(D),
                first, first_t, first, first, nxt, nxt_t, nxt, nxt, gspec, gspec, _const_spec(gh.shape),
                _const_spec(wfo.shape), _const_spec(wmo.shape), _const_spec(wo.shape), _const_spec(gffn.shape),
                _const_spec(wg.shape), _const_spec(wu.shape), _const_spec(wd.shape)]
    return pl.pallas_call(
        functools.partial(_tail_kernel, L=L, heads=heads, nchunk=tm // L),
        grid=(B, nt), in_specs=in_specs, out_specs=tok(D),
        out_shape=jax.ShapeDtypeStruct((B, S, D), x.dtype),
        scratch_shapes=[pltpu.VMEM((heads, LANES, 2 * LANES), F32), pltpu.VMEM((heads, 1, LANES), F32),
                        pltpu.VMEM((2, tm, W), BF16)],
        compiler_params=pltpu.CompilerParams(
            dimension_semantics=("arbitrary", "arbitrary"), vmem_limit_bytes=TAIL_VMEM_LIMIT_BYTES),
        name="tail",
    )(x, ya, mga, mgb, qb, kbt, vb, og, qb, kbt, vb, og, bcs, li, gh, wfo, wmo, wo, gffn, wg, wu, wd)


def _block(x, g_mix, w_in, b_fox_f, g_q_fox, g_k_fox, conv_w, conv_b, b_mlstm_i, b_mlstm_f,
           g_mlstm_h, w_fox_out, w_mlstm_out, w_o, g_ffn, w_gate, w_up, w_down):
    B, S, D = x.shape
    fw = w_fox_out.shape[0]
    mw = w_mlstm_out.shape[0]
    fh = fw // FOX_HEAD_DIM
    assert fh == 8 and mw // LANES == MLSTM_HEADS and fw % LANES == 0
    tm = min(512, S)
    t_attn = min(512, S)
    chunk = min(128, S)
    assert S % tm == 0 and S % t_attn == 0 and tm % chunk == 0

    sizes = (fw, fw, fw, fh, mw, mw, mw, MLSTM_HEADS, MLSTM_HEADS, mw, D, D)
    offs = [0]
    for sz in sizes:
        offs.append(offs[-1] + sz)
    col = lambda k: w_in[:, offs[k]:offs[k + 1]]
    w_main = jnp.concatenate([col(0), col(1), col(2), col(4), col(5), col(6), col(9), col(10), col(11)],
                             axis=1).astype(BF16)
    w_gt = jnp.concatenate([col(3), col(7), col(8)], axis=1).T.astype(BF16)
    gate_bias = jnp.concatenate([b_fox_f, b_mlstm_i, b_mlstm_f]).astype(F32)[:, None]
    root_d = FOX_HEAD_DIM ** 0.5
    gq = jnp.tile(g_q_fox.astype(F32) * (root_d * FOX_HEAD_DIM ** -0.5 * LOG2E), fh)[None, :]
    gk = jnp.tile(g_k_fox.astype(F32) * root_d, fh)[None, :]

    (qa, ka, va, qb, kbt, vb, og, mga, mgb, gates) = _proj_call(
        x, g_mix.astype(F32)[None, :], w_main, w_gt, gate_bias, gq, gk,
        conv_w.astype(F32), conv_b.astype(F32)[None, :], tm=tm, nsub=2 if tm % 512 == 0 else 1)

    scans = _scan_call(gates, chunk=chunk)
    c4 = scans[:, :fh].reshape(B, fh, S // t_attn, t_attn)
    li = scans[:, fh:fh + MLSTM_HEADS].reshape(B, MLSTM_HEADS, S // chunk, chunk)
    bcs = scans[:, fh + MLSTM_HEADS:].reshape(B, MLSTM_HEADS, S // chunk, chunk)

    qk_bound = (FOX_HEAD_DIM * (FOX_HEAD_DIM ** -0.5 * LOG2E) * 1.02
                * jnp.max(jnp.abs(g_q_fox.astype(F32))) * jnp.max(jnp.abs(g_k_fox.astype(F32))))
    cstart = jnp.concatenate([c4[..., 0].reshape(-1), qk_bound[None]])
    ya = _fox_call(qa, ka, va, c4, cstart, t=t_attn)
    return _tail_call(x, ya, mga, mgb, qb, kbt, vb, og, bcs, li, g_mlstm_h.astype(F32)[None, :],
                      w_fox_out.astype(BF16), w_mlstm_out.astype(BF16), w_o.astype(BF16),
                      g_ffn.astype(F32)[None, :], w_gate.astype(BF16), w_up.astype(BF16),
                      w_down.astype(BF16), tm=tm, L=chunk)


def kernel(x, g_mix, w_in, b_fox_f, g_q_fox, g_k_fox, conv_w, conv_b, b_mlstm_i, b_mlstm_f, g_mlstm_h,
           w_fox_out, w_mlstm_out, w_o, g_ffn, w_gate, w_up, w_down):
    for l in range(g_mix.shape[0]):
        x = _block(x, g_mix[l], w_in[l], b_fox_f[l], g_q_fox[l], g_k_fox[l], conv_w[l], conv_b[l],
                   b_mlstm_i[l], b_mlstm_f[l], g_mlstm_h[l], w_fox_out[l], w_mlstm_out[l], w_o[l],
                   g_ffn[l], w_gate[l], w_up[l], w_down[l])
    return x
```

```python
import functools

import jax
import jax.numpy as jnp
from jax import lax
from jax.experimental import pallas as pl
from jax.experimental.pallas import tpu as pltpu

EPS = 1e-6
FOX_HEAD_DIM = 64
MLSTM_HEADS = 4
CONV_WIDTH = 4
LANES = 128
MXU_TILE = 256
NEG_BIG = -1e30
LOG2E = 1.4426950408889634
SAFE_EXP2_RANGE = 100.0
VMEM_LIMIT_BYTES = 56 * 1024 * 1024
TAIL_VMEM_LIMIT_BYTES = 60 * 1024 * 1024

F32 = jnp.float32
BF16 = jnp.bfloat16


def _const_spec(shape):
    nd = len(shape)
    return pl.BlockSpec(shape, lambda *_: (0,) * nd, pipeline_mode=pl.Buffered(1))


def _exp(x):
    return jnp.exp2(x * LOG2E)


def _log_sigmoid(z):
    return jnp.minimum(z, 0.0) - jnp.log(1.0 + _exp(-jnp.abs(z)))


def _sigmoid(z):
    return 1.0 / (1.0 + jnp.exp2(z * (-LOG2E)))


def _head_rms(x, g_row, lo_mask):
    outs = []
    for j in range(x.shape[1] // LANES):
        v = x[:, j * LANES:(j + 1) * LANES]
        v2 = v * v
        s_lo = jnp.sum(jnp.where(lo_mask, v2, 0.0), axis=-1, keepdims=True)
        s_hi = jnp.sum(jnp.where(lo_mask, 0.0, v2), axis=-1, keepdims=True)
        outs.append(v * lax.rsqrt(jnp.where(lo_mask, s_lo, s_hi) + FOX_HEAD_DIM * EPS))
    return jnp.concatenate(outs, axis=-1) * g_row


def _proj_kernel(x_ref, gmix_ref, w_ref, wg_ref, gb_ref, gq_ref, gk_ref, cw_ref, cb_ref,
                 qa_ref, ka_ref, va_ref, qb_ref, kbt_ref, vb_ref, og_ref, mga_ref, mgb_ref, gates_ref,
                 cbuf_ref, *, tm, nsub, fw, mw, d_model, k_scale):
    i = pl.program_id(1)
    lane = lax.broadcasted_iota(jnp.int32, (1, LANES), 1)
    lo_mask = lane < FOX_HEAD_DIM

    @pl.when(i == 0)
    def _():
        cbuf_ref[0:8, :] = jnp.zeros((8, 2 * mw), F32)

    ts = tm // nsub
    for r in range(nsub):
        rows = slice(r * ts, (r + 1) * ts)
        c0 = 8 + r * ts
        x = x_ref[rows, :]
        h = x * lax.rsqrt(jnp.mean(x * x, axis=-1, keepdims=True) + EPS) * gmix_ref[...]
        hb = h.astype(BF16)

        def proj(lo, width):
            return jnp.dot(hb, w_ref[:, lo:lo + width], preferred_element_type=F32)

        o_qb = 3 * fw
        cbuf_ref[c0:c0 + ts, 0:mw] = proj(o_qb, mw)
        cbuf_ref[c0:c0 + ts, mw:2 * mw] = proj(o_qb + mw, mw)
        qa_ref[rows, :] = _head_rms(proj(0, fw), gq_ref[...], lo_mask).astype(BF16)
        ka_ref[rows, :] = _head_rms(proj(fw, fw), gk_ref[...], lo_mask).astype(BF16)
        va_ref[rows, :] = proj(2 * fw, fw).astype(BF16)
        vb_ref[rows, :] = proj(o_qb + 2 * mw, mw).astype(BF16)
        og_ref[rows, :] = _sigmoid(proj(o_qb + 3 * mw, mw)).astype(BF16)
        mga_ref[rows, :] = _sigmoid(proj(o_qb + 4 * mw, d_model)).astype(BF16)
        mgb_ref[rows, :] = _sigmoid(proj(o_qb + 4 * mw + d_model, d_model)).astype(BF16)

        z = lax.dot_general(wg_ref[...], hb, (((1,), (1,)), ((), ())), preferred_element_type=F32) + gb_ref[...]
        row = lax.broadcasted_iota(jnp.int32, z.shape, 0)
        is_input_gate = (row >= 8) & (row < 12)
        gates_ref[:, rows] = jnp.where(is_input_gate, z, _log_sigmoid(z))

        for half in range(2):
            cols = slice(half * mw, (half + 1) * mw)
            y = cb_ref[:, cols] + cw_ref[CONV_WIDTH - 1:CONV_WIDTH, cols] * cbuf_ref[c0:c0 + ts, cols]
            for j in range(CONV_WIDTH - 1):
                r0 = c0 - (CONV_WIDTH - 1) + j
                y = y + cw_ref[j:j + 1, cols] * cbuf_ref[r0:r0 + ts, cols]
            y = y * _sigmoid(y)
            if half == 0:
                qb_ref[rows, :] = y.astype(BF16)
            else:
                kbt_ref[:, rows] = (y * k_scale).T.astype(BF16)
    cbuf_ref[0:8, :] = cbuf_ref[tm:tm + 8, :]


def _proj_call(x, gmix, w_main, w_gt, gate_bias, gq, gk, conv_w, conv_b, *, tm, nsub):
    B, S, D = x.shape
    fw = gq.shape[1]
    mw = conv_w.shape[1] // 2
    n_main = w_main.shape[1]
    grid = (B, S // tm)
    tok = lambda width: pl.BlockSpec((None, tm, width), lambda b, i: (b, i, 0))
    out_shape = (
        jax.ShapeDtypeStruct((B, S, fw), BF16),
        jax.ShapeDtypeStruct((B, S, fw), BF16),
        jax.ShapeDtypeStruct((B, S, fw), BF16),
        jax.ShapeDtypeStruct((B, S, mw), BF16),
        jax.ShapeDtypeStruct((B, mw, S), BF16),
        jax.ShapeDtypeStruct((B, S, mw), BF16),
        jax.ShapeDtypeStruct((B, S, mw), BF16),
        jax.ShapeDtypeStruct((B, S, D), BF16),
        jax.ShapeDtypeStruct((B, S, D), BF16),
        jax.ShapeDtypeStruct((B, 16, S), F32),
    )
    out_specs = (
        tok(fw), tok(fw), tok(fw), tok(mw),
        pl.BlockSpec((None, mw, tm), lambda b, i: (b, 0, i)),
        tok(mw), tok(mw), tok(D), tok(D),
        pl.BlockSpec((None, 16, tm), lambda b, i: (b, 0, i)),
    )
    in_specs = [
        tok(D),
        _const_spec((1, D)),
        _const_spec((D, n_main)),
        _const_spec((16, D)),
        _const_spec((16, 1)),
        _const_spec((1, fw)),
        _const_spec((1, fw)),
        _const_spec((CONV_WIDTH, 2 * mw)),
        _const_spec((1, 2 * mw)),
    ]
    kern = functools.partial(_proj_kernel, tm=tm, nsub=nsub, fw=fw, mw=mw, d_model=D,
                             k_scale=(mw // MLSTM_HEADS) ** -0.5)
    return pl.pallas_call(
        kern, grid=grid, in_specs=in_specs, out_specs=out_specs, out_shape=out_shape,
        scratch_shapes=[pltpu.VMEM((tm + 8, 2 * mw), F32)],
        compiler_params=pltpu.CompilerParams(
            dimension_semantics=("arbitrary", "arbitrary"), vmem_limit_bytes=VMEM_LIMIT_BYTES),
        name="proj",
    )(x, gmix, w_main, w_gt, gate_bias, gq, gk, conv_w, conv_b)


def _scan_kernel(g_ref, o_ref, *, chunk):
    x = g_ref[...]
    S = x.shape[1]
    lane = lax.broadcasted_iota(jnp.int32, x.shape, 1)
    row = lax.broadcasted_iota(jnp.int32, x.shape, 0)
    glob = x
    sh = 1
    while sh < S:
        glob = glob + jnp.where(lane >= sh, pltpu.roll(glob, sh, axis=1), 0.0)
        sh *= 2
    seg = x
    lane_in = lane & (chunk - 1)
    sh = 1
    while sh < chunk:
        seg = seg + jnp.where(lane_in >= sh, pltpu.roll(seg, sh, axis=1), 0.0)
        sh *= 2
    o_ref[...] = jnp.where(row < 8, glob, jnp.where(row < 12, x, seg))


def _scan_call(gates, *, chunk):
    B, R, S = gates.shape
    spec = pl.BlockSpec((None, R, S), lambda b: (b, 0, 0))
    return pl.pallas_call(
        functools.partial(_scan_kernel, chunk=chunk),
        grid=(B,), in_specs=[spec], out_specs=spec,
        out_shape=jax.ShapeDtypeStruct(gates.shape, F32),
        compiler_params=pltpu.CompilerParams(dimension_semantics=("arbitrary",)),
        name="gate_scan",
    )(gates)


def _split3(x):
    hi = x.astype(BF16).astype(F32)
    r = x - hi
    mid = r.astype(BF16).astype(F32)
    lo = (r - mid).astype(BF16).astype(F32)
    return hi, mid, lo


def _fox_kernel(cs_ref, q_ref, k_ref, v_ref, c_ref, o_ref,
                kaug_ref, vtaug_ref, qt_ref, acc_ref, m_ref, s0_ref, s1_ref, p0_ref, p1_ref, *, t, nk, heads):
    b = pl.program_id(0)
    i = pl.program_id(1)
    cs_base = b * heads * nk
    half = FOX_HEAD_DIM
    pairs = heads // 2

    def cs(h, j):
        return cs_ref[cs_base + h * nk + j]

    bound = cs_ref[pl.num_programs(0) * heads * nk]
    safe = 2.0 * bound < SAFE_EXP2_RANGE

    row = lax.broadcasted_iota(jnp.int32, (LANES, t), 0)
    top = row < half
    lane = lax.broadcasted_iota(jnp.int32, (1, LANES), 1)
    lo_mask = lane < half
    row8 = lax.broadcasted_iota(jnp.int32, (8, t), 0)
    ones_a = jnp.where(row < half + 3, 1.0, 0.0)
    ones_b = jnp.where(row < 3, 1.0, 0.0)
    pad = jnp.zeros((half - 8, t), F32)

    def causal_mask(s):
        kr = lax.broadcasted_iota(jnp.int32, (t, t), 0)
        qc = lax.broadcasted_iota(jnp.int32, (t, t), 1)
        return jnp.where(kr <= qc, s, NEG_BIG)

    def shift_row(h, j):
        return (cs(h, j) - c_ref[h, pl.ds(i, 1), :]) * LOG2E + bound

    def prep_pair(pp):
        lanes = slice(pp * LANES, (pp + 1) * LANES)
        q_t = q_ref[:, lanes].astype(F32).T
        qs = (jnp.where(top, q_t, ones_a).astype(BF16), jnp.where(top, ones_b, q_t).astype(BF16))
        pieces = []
        for e in range(2):
            h = 2 * pp + e
            qt_ref[h] = qs[e]
            brel = (cs(h, i) - c_ref[h, pl.ds(i, 1), :]) * LOG2E
            hi, mid, lo = _split3(brel)
            pieces.append(jnp.where(row8 == 0, hi, jnp.where(row8 == 1, mid, jnp.where(row8 == 2, lo, 0.0))))
        extra_t = jnp.concatenate([pieces[1], pad, pieces[0], pad], axis=0).T.astype(BF16)
        k2 = k_ref[:, lanes]
        ks = (jnp.where(lo_mask, k2, extra_t), jnp.where(lo_mask, extra_t, k2))
        v_t = v_ref[:, lanes].astype(F32).T.astype(BF16)
        one = jnp.ones_like(v_t)
        vs = (jnp.where(top, v_t, one), jnp.where(top, one, v_t))
        for e in range(2):
            kaug_ref[2 * pp + e, i] = ks[e]
            vtaug_ref[2 * pp + e, i] = vs[e]
        return qs, ks, vs

    @pl.when(safe)
    def _():
        hq = t // 2
        lo, hi = slice(0, hq), slice(hq, t)
        kr = lax.broadcasted_iota(jnp.int32, (hq, hq), 0)
        qc = lax.broadcasted_iota(jnp.int32, (hq, hq), 1)
        tri = kr <= qc
        dot = functools.partial(jnp.dot, preferred_element_type=F32)

        def diag_scores(pp):
            qs, ks, vs = prep_pair(pp)
            out = []
            for e in range(2):
                s00 = jnp.where(tri, dot(ks[e][lo], qs[e][:, lo]), NEG_BIG)
                s01 = dot(ks[e][lo], qs[e][:, hi])
                s11 = jnp.where(tri, dot(ks[e][hi], qs[e][:, hi]), NEG_BIG)
                out.append((s00, s01, s11, vs[e]))
            return out

        def diag_values(pp, scores):
            for e, (s00, s01, s11, v_e) in enumerate(scores):
                h = 2 * pp + e
                shift = shift_row(h, i)
                p00 = jnp.exp2(s00 - shift[:, lo]).astype(BF16)
                p_hi = jnp.concatenate([jnp.exp2(s01 - shift[:, hi]), jnp.exp2(s11 - shift[:, hi])],
                                       axis=0).astype(BF16)
                acc_ref[h, :, lo] = dot(v_e[:, lo], p00)
                acc_ref[h, :, hi] = dot(v_e, p_hi)

        pending = diag_scores(0)
        for pp in range(1, pairs):
            nxt = diag_scores(pp)
            diag_values(pp - 1, pending)
            pending = nxt
        diag_values(pairs - 1, pending)

        def pexp(j, p_ref):
            for h in range(heads):
                s = jnp.dot(kaug_ref[h, j], qt_ref[h], preferred_element_type=F32)
                p_ref[h] = jnp.exp2(s - shift_row(h, j)).astype(BF16)

        def pv(j, p_ref):
            for h in range(heads):
                acc_ref[h] += jnp.dot(vtaug_ref[h, j], p_ref[h], preferred_element_type=F32)

        @pl.when(i >= 1)
        def _():
            pexp(0, p0_ref)

            def pair_body(n, carry):
                j = 2 * n
                pexp(j + 1, p1_ref)
                pv(j, p0_ref)
                pexp(j + 2, p0_ref)
                pv(j + 1, p1_ref)
                return carry

            lax.fori_loop(0, (i - 1) // 2, pair_body, 0)

            @pl.when(i % 2 == 1)
            def _():
                pv(i - 1, p0_ref)

            @pl.when(i % 2 == 0)
            def _():
                pexp(i - 1, p1_ref)
                pv(i - 2, p0_ref)
                pv(i - 1, p1_ref)

    @pl.when(jnp.logical_not(safe))
    def _():
        acc_ref[...] = jnp.zeros(acc_ref.shape, F32)
        for pp in range(pairs):
            prep_pair(pp)

        def one_pair(pp, carry):
            def scores(j, s_ref):
                for e in range(2):
                    s_ref[e] = jnp.dot(kaug_ref[2 * pp + e, j], qt_ref[2 * pp + e], preferred_element_type=F32)

            def consume(j, s_ref, masked):
                for e in range(2):
                    h = 2 * pp + e
                    s = s_ref[e]
                    if masked:
                        s = causal_mask(s)
                    d = (cs(h, j) - cs(h, i)) * LOG2E
                    m_old = m_ref[e]
                    m_new = jnp.maximum(m_old, jnp.max(s, axis=0, keepdims=True) - d)
                    alpha = jnp.exp2(m_old - m_new)
                    p = jnp.exp2(s - (m_new + d))
                    acc_ref[h] = alpha * acc_ref[h] + jnp.dot(vtaug_ref[h, j], p.astype(BF16),
                                                             preferred_element_type=F32)
                    m_ref[e] = m_new

            m_ref[...] = jnp.full(m_ref.shape, NEG_BIG, F32)
            scores(0, s0_ref)

            def pair_body(n, c2):
                j = 2 * n
                scores(j + 1, s1_ref)
                consume(j, s0_ref, False)
                scores(j + 2, s0_ref)
                consume(j + 1, s1_ref, False)
                return c2

            lax.fori_loop(0, i // 2, pair_body, 0)

            @pl.when(i % 2 == 0)
            def _():
                consume(i, s0_ref, True)

            @pl.when(i % 2 == 1)
            def _():
                scores(i, s1_ref)
                consume(i - 1, s0_ref, False)
                consume(i, s1_ref, True)

            return carry

        lax.fori_loop(0, pairs, one_pair, 0)

    for pp in range(pairs):
        a0 = acc_ref[2 * pp]
        a1 = acc_ref[2 * pp + 1]
        out_t = jnp.concatenate([a0[:half] / a0[half:], a1[half:] / a1[:half]], axis=0)
        o_ref[:, pp * LANES:(pp + 1) * LANES] = out_t.T.astype(o_ref.dtype)


def _fox_call(qa, ka, va, c4, cstart, *, t):
    B, S, W = qa.shape
    heads = W // FOX_HEAD_DIM
    nk = S // t
    tile = pl.BlockSpec((None, t, W), lambda b, i: (b, i, 0))
    cspec = pl.BlockSpec((None, heads, nk, t), lambda b, i: (b, 0, 0, 0))
    return pl.pallas_call(
        functools.partial(_fox_kernel, t=t, nk=nk, heads=heads),
        grid=(B, nk),
        in_specs=[pl.BlockSpec(memory_space=pltpu.SMEM), tile, tile, tile, cspec],
        out_specs=tile,
        out_shape=jax.ShapeDtypeStruct((B, S, W), BF16),
        scratch_shapes=[pltpu.VMEM((heads, nk, t, LANES), BF16), pltpu.VMEM((heads, nk, LANES, t), BF16),
                        pltpu.VMEM((heads, LANES, t), BF16), pltpu.VMEM((heads, LANES, t), F32),
                        pltpu.VMEM((2, 1, t), F32),
                        pltpu.VMEM((2, t, t), F32), pltpu.VMEM((2, t, t), F32),
                        pltpu.VMEM((heads, t, t), BF16), pltpu.VMEM((heads, t, t), BF16)],
        compiler_params=pltpu.CompilerParams(
            dimension_semantics=("arbitrary", "arbitrary"), vmem_limit_bytes=VMEM_LIMIT_BYTES),
        name="fox_attn",
    )(cstart, qa, ka, va, c4)


def _mlstm_stages(q_ref, kt_ref, v_ref, og_ref, b_ref, li_ref, gh_ref, c_ref, m_ref, out_ref, tile, fresh, *,
                  L, heads, nchunk):
    d = LANES
    r = lax.broadcasted_iota(jnp.int32, (L, L), 0)
    c = lax.broadcasted_iota(jnp.int32, (L, L), 1)
    causal = c <= r
    eye = c == r
    ones = jnp.ones((L, d), BF16)
    items = [(cc, hh) for cc in range(nchunk) for hh in range(heads)]
    col = lambda hh: slice(hh * d, (hh + 1) * d)
    rows = lambda cc: slice(cc * L, (cc + 1) * L)
    q_of = lambda it: q_ref[rows(it[0]), col(it[1])]
    kt_of = lambda it: kt_ref[col(it[1]), rows(it[0])]
    v_aug_of = lambda it: jnp.concatenate([v_ref[rows(it[0]), col(it[1])], ones], axis=1)
    st = {}

    def stage_scores():
        st["s_raw"] = {it: jnp.dot(q_of(it), kt_of(it), preferred_element_type=F32) for it in items}

    def stage_updates():
        b_row, u_row = {}, {}
        for cc, hh in items:
            ci = tile * nchunk + cc
            b_row[cc, hh] = b_ref[hh, pl.ds(ci, 1), :]
            u_row[cc, hh] = li_ref[hh, pl.ds(ci, 1), :] - b_row[cc, hh]
        st["b_row"], st["u_row"] = b_row, u_row
        st["b_last"] = {it: b_row[it][:, L - 1:L] for it in items}
        u_max = {it: jnp.max(u_row[it], axis=1, keepdims=True) for it in items}
        st["m_loc"] = {it: st["b_last"][it] + u_max[it] for it in items}
        kw = {it: (kt_of(it).astype(F32) * _exp(u_row[it] - u_max[it])).astype(BF16) for it in items}
        st["upd"] = {it: jnp.dot(kw[it], v_aug_of(it), preferred_element_type=F32) for it in items}

    def stage_intra():
        b_col = {it: jnp.sum(jnp.where(eye, st["b_row"][it], 0.0), axis=1, keepdims=True) for it in items}
        dmat = {it: jnp.where(causal, b_col[it] + st["u_row"][it], NEG_BIG) for it in items}
        st["b_col"] = b_col
        st["m_d"] = {it: jnp.max(dmat[it], axis=1, keepdims=True) for it in items}
        s = {it: (st["s_raw"][it] * _exp(dmat[it] - st["m_d"][it])).astype(BF16) for it in items}
        st["intra"] = {it: jnp.dot(s[it], v_aug_of(it), preferred_element_type=F32) for it in items}

    def stage_inter():
        c_state = [jnp.where(fresh, 0.0, c_ref[hh]) for hh in range(heads)]
        m_prev = [jnp.where(fresh, 0.0, m_ref[hh]) for hh in range(heads)]
        st["m_in"], c_in = {}, {}
        for cc, hh in items:
            st["m_in"][cc, hh] = m_prev[hh]
            c_in[cc, hh] = c_state[hh]
            m_new = jnp.maximum(st["b_last"][cc, hh] + m_prev[hh], st["m_loc"][cc, hh])
            decay = _exp(st["b_last"][cc, hh] + m_prev[hh] - m_new)
            gain = _exp(st["m_loc"][cc, hh] - m_new)
            c_state[hh] = jnp.tile(decay, (1, 2)) * c_state[hh] + jnp.tile(gain, (1, 2)) * st["upd"][cc, hh]
            m_prev[hh] = m_new
        for hh in range(heads):
            c_ref[hh] = c_state[hh]
            m_ref[hh] = m_prev[hh]
        st["inter"] = {it: jnp.dot(q_of(it), c_in[it].astype(BF16), preferred_element_type=F32)
                       for it in items}

    def stage_output():
        for cc, hh in items:
            a = st["b_col"][cc, hh] + st["m_in"][cc, hh]
            m_t = jnp.maximum(a, st["m_d"][cc, hh])
            w_inter = _exp(a - m_t)
            w_intra = _exp(st["m_d"][cc, hh] - m_t)
            inter, intra = st["inter"][cc, hh], st["intra"][cc, hh]
            num = w_inter * inter[:, :d] + w_intra * intra[:, :d]
            den = w_inter * inter[:, d:] + w_intra * intra[:, d:]
            hval = num / jnp.maximum(jnp.abs(den), _exp(-m_t))
            mu = jnp.mean(hval, axis=-1, keepdims=True)
            xc = hval - mu
            yn = xc * lax.rsqrt(jnp.mean(xc * xc, axis=-1, keepdims=True) + EPS)
            out_ref[rows(cc), col(hh)] = (yn * gh_ref[:, col(hh)]
                                          * og_ref[rows(cc), col(hh)].astype(F32)).astype(out_ref.dtype)

    return [stage_scores, stage_updates, stage_intra, stage_inter, stage_output]


def _tail_kernel(x_ref, ya_ref, mga_ref, mgb_ref, q0_ref, kt0_ref, v0_ref, og0_ref,
                 qn_ref, ktn_ref, vn_ref, ogn_ref, b_ref, li_ref, gh_ref,
                 wfo_ref, wmo_ref, wo_ref, gffn_ref, wg_ref, wu_ref, wd_ref, o_ref,
                 c_ref, m_ref, yb_ref, *, L, heads, nchunk):
    b = pl.program_id(0)
    i = pl.program_id(1)
    row_end = i == pl.num_programs(1) - 1
    cur = (b * pl.num_programs(1) + i) % 2
    kw = dict(L=L, heads=heads, nchunk=nchunk)

    @pl.when((b == 0) & (i == 0))
    def _():
        for stage in _mlstm_stages(q0_ref, kt0_ref, v0_ref, og0_ref, b_ref, li_ref, gh_ref, c_ref, m_ref,
                                   yb_ref.at[0], 0, True, **kw):
            stage()

    stages = _mlstm_stages(qn_ref, ktn_ref, vn_ref, ogn_ref, b_ref, li_ref, gh_ref, c_ref, m_ref,
                           yb_ref.at[1 - cur], jnp.where(row_end, 0, i + 1), row_end, **kw)

    pa = jnp.dot(ya_ref[...], wfo_ref[...], preferred_element_type=F32)
    stages[0]()
    pb = jnp.dot(yb_ref[cur], wmo_ref[...], preferred_element_type=F32)
    stages[1]()
    merged = mga_ref[...].astype(F32) * pa + mgb_ref[...].astype(F32) * pb
    x1 = x_ref[...] + jnp.dot(merged.astype(BF16), wo_ref[...], preferred_element_type=F32)
    stages[2]()
    h2 = (x1 * lax.rsqrt(jnp.mean(x1 * x1, axis=-1, keepdims=True) + EPS) * gffn_ref[...]).astype(BF16)
    d_ff = wg_ref.shape[1]
    slab = MXU_TILE
    bounds = [(lo, min(lo + slab, d_ff)) for lo in range(0, d_ff, slab)]
    after = {3: stages[3], len(bounds) // 2 + 3: stages[4]}
    o_ref[...] = x1
    for n, (lo, hi) in enumerate(bounds):
        gate = jnp.dot(h2, wg_ref[:, lo:hi], preferred_element_type=F32)
        up = jnp.dot(h2, wu_ref[:, lo:hi], preferred_element_type=F32)
        act = (gate * _sigmoid(gate) * up).astype(BF16)
        o_ref[...] += jnp.dot(act, wd_ref[lo:hi, :], preferred_element_type=F32)
        if n in after:
            after[n]()


def _tail_call(x, ya, mga, mgb, qb, kbt, vb, og, bcs, li, gh, wfo, wmo, wo, gffn, wg, wu, wd, *, tm, L):
    B, S, D = x.shape
    W = qb.shape[2]
    heads = W // LANES
    nt = S // tm
    tok = lambda width: pl.BlockSpec((None, tm, width), lambda b, i: (b, i, 0))
    first = pl.BlockSpec((None, tm, W), lambda b, i: (0, 0, 0), pipeline_mode=pl.Buffered(1))
    first_t = pl.BlockSpec((None, W, tm), lambda b, i: (0, 0, 0), pipeline_mode=pl.Buffered(1))
    n_row = lambda b, i: jnp.where(i == nt - 1, jnp.minimum(b + 1, B - 1), b)
    n_tile = lambda b, i: jnp.where(i == nt - 1, 0, i + 1)
    nxt = pl.BlockSpec((None, tm, W), lambda b, i: (n_row(b, i), n_tile(b, i), 0))
    nxt_t = pl.BlockSpec((None, W, tm), lambda b, i: (n_row(b, i), 0, n_tile(b, i)))
    gspec = pl.BlockSpec((None, heads, S // L, L), lambda b, i: (n_row(b, i), 0, 0, 0))
    in_specs = [tok(D), tok(ya.shape[2]), tok(D), tok(D),
                first, first_t, first, first, nxt, nxt_t, nxt, nxt, gspec, gspec, _const_spec(gh.shape),
                _const_spec(wfo.shape), _const_spec(wmo.shape), _const_spec(wo.shape), _const_spec(gffn.shape),
                _const_spec(wg.shape), _const_spec(wu.shape), _const_spec(wd.shape)]
    return pl.pallas_call(
        functools.partial(_tail_kernel, L=L, heads=heads, nchunk=tm // L),
        grid=(B, nt), in_specs=in_specs, out_specs=tok(D),
        out_shape=jax.ShapeDtypeStruct((B, S, D), x.dtype),
        scratch_shapes=[pltpu.VMEM((heads, LANES, 2 * LANES), F32), pltpu.VMEM((heads, 1, LANES), F32),
                        pltpu.VMEM((2, tm, W), BF16)],
        compiler_params=pltpu.CompilerParams(
            dimension_semantics=("arbitrary", "arbitrary"), vmem_limit_bytes=TAIL_VMEM_LIMIT_BYTES),
        name="tail",
    )(x, ya, mga, mgb, qb, kbt, vb, og, qb, kbt, vb, og, bcs, li, gh, wfo, wmo, wo, gffn, wg, wu, wd)


def _block(x, g_mix, w_in, b_fox_f, g_q_fox, g_k_fox, conv_w, conv_b, b_mlstm_i, b_mlstm_f,
           g_mlstm_h, w_fox_out, w_mlstm_out, w_o, g_ffn, w_gate, w_up, w_down):
    B, S, D = x.shape
    fw = w_fox_out.shape[0]
    mw = w_mlstm_out.shape[0]
    fh = fw // FOX_HEAD_DIM
    assert fh == 8 and mw // LANES == MLSTM_HEADS and fw % LANES == 0
    tm = min(512, S)
    t_attn = min(512, S)
    chunk = min(128, S)
    assert S % tm == 0 and S % t_attn == 0 and tm % chunk == 0

    sizes = (fw, fw, fw, fh, mw, mw, mw, MLSTM_HEADS, MLSTM_HEADS, mw, D, D)
    offs = [0]
    for sz in sizes:
        offs.append(offs[-1] + sz)
    col = lambda k: w_in[:, offs[k]:offs[k + 1]]
    w_main = jnp.concatenate([col(0), col(1), col(2), col(4), col(5), col(6), col(9), col(10), col(11)],
                             axis=1).astype(BF16)
    w_gt = jnp.concatenate([col(3), col(7), col(8)], axis=1).T.astype(BF16)
    gate_bias = jnp.concatenate([b_fox_f, b_mlstm_i, b_mlstm_f]).astype(F32)[:, None]
    root_d = FOX_HEAD_DIM ** 0.5
    gq = jnp.tile(g_q_fox.astype(F32) * (root_d * FOX_HEAD_DIM ** -0.5 * LOG2E), fh)[None, :]
    gk = jnp.tile(g_k_fox.astype(F32) * root_d, fh)[None, :]

    (qa, ka, va, qb, kbt, vb, og, mga, mgb, gates) = _proj_call(
        x, g_mix.astype(F32)[None, :], w_main, w_gt, gate_bias, gq, gk,
        conv_w.astype(F32), conv_b.astype(F32)[None, :], tm=tm, nsub=2 if tm % 512 == 0 else 1)

    scans = _scan_call(gates, chunk=chunk)
    c4 = scans[:, :fh].reshape(B, fh, S // t_attn, t_attn)
    li = scans[:, fh:fh + MLSTM_HEADS].reshape(B, MLSTM_HEADS, S // chunk, chunk)
    bcs = scans[:, fh + MLSTM_HEADS:].reshape(B, MLSTM_HEADS, S // chunk, chunk)

    qk_bound = (FOX_HEAD_DIM * (FOX_HEAD_DIM ** -0.5 * LOG2E) * 1.02
                * jnp.max(jnp.abs(g_q_fox.astype(F32))) * jnp.max(jnp.abs(g_k_fox.astype(F32))))
    cstart = jnp.concatenate([c4[..., 0].reshape(-1), qk_bound[None]])
    ya = _fox_call(qa, ka, va, c4, cstart, t=t_attn)
    return _tail_call(x, ya, mga, mgb, qb, kbt, vb, og, bcs, li, g_mlstm_h.astype(F32)[None, :],
                      w_fox_out.astype(BF16), w_mlstm_out.astype(BF16), w_o.astype(BF16),
                      g_ffn.astype(F32)[None, :], w_gate.astype(BF16), w_up.astype(BF16),
                      w_down.astype(BF16), tm=tm, L=chunk)


def kernel(x, g_mix, w_in, b_fox_f, g_q_fox, g_k_fox, conv_w, conv_b, b_mlstm_i, b_mlstm_f, g_mlstm_h,
           w_fox_out, w_mlstm_out, w_o, g_ffn, w_gate, w_up, w_down):
    for l in range(g_mix.shape[0]):
        x = _block(x, g_mix[l], w_in[l], b_fox_f[l], g_q_fox[l], g_k_fox[l], conv_w[l], conv_b[l],
                   b_mlstm_i[l], b_mlstm_f[l], g_mlstm_h[l], w_fox_out[l], w_mlstm_out[l], w_o[l],
                   g_ffn[l], w_gate[l], w_up[l], w_down[l])
    return x
```
